```python
import math
import jax, jax.numpy as jnp
from jax import lax
import numpy as np

D_MODEL = 1024
BATCH = 32
SEQ = 2048
DEPTH = 2
DEC_BATCH = 2
DEC_SEQ = 16384
PAST_LEN = 128

N_META = 16
GRID_W = 64
Q_BLOCK = 128
NORM_EPS = 1e-6
ROPE_THETA = 10000.0
D_FF = 2816

MLA_HEADS = 6
MLA_Q_LORA = 256
MLA_KV_LORA = 128
MLA_NOPE = 64
MLA_ROPE = 32
MLA_V = 64

GQA_HEADS = 6
GQA_KV_HEADS = 2
GQA_HEAD_DIM = 64

GLA_HEADS = 4
GLA_DK = 32
GLA_DV = 64
GLA_GATE_RANK = 16
GLA_GATE_NORM = 16.0
GLA_CHUNK = 64

MIX_W = MLA_HEADS * MLA_V + GQA_HEADS * GQA_HEAD_DIM + GLA_HEADS * GLA_DV
IN_SPLITS = (MLA_Q_LORA, MLA_KV_LORA, MLA_ROPE,
             GQA_HEADS * GQA_HEAD_DIM, GQA_KV_HEADS * GQA_HEAD_DIM, GQA_KV_HEADS * GQA_HEAD_DIM,
             GLA_HEADS * GLA_DK, GLA_HEADS * GLA_DK, GLA_HEADS * GLA_DV,
             GLA_GATE_RANK, GLA_GATE_RANK, GLA_HEADS * GLA_DV)
IN_COLS = sum(IN_SPLITS)

kernel_name = 'hymba_mla_gqa_gla_macaron_encoder'


def rms_norm(x, w):
    xf = x.astype(jnp.float32)
    y = xf * lax.rsqrt(jnp.mean(xf * xf, axis=-1, keepdims=True) + NORM_EPS)
    return (y * w.astype(jnp.float32)).astype(x.dtype)


def swiglu(h, w_in, w_out):
    gate, up = jnp.split(h @ w_in, 2, axis=-1)
    return (jax.nn.silu(gate) * up) @ w_out


def split_cols(t, sizes):
    idx, acc = [], 0
    for s in sizes[:-1]:
        acc += s
        idx.append(acc)
    return jnp.split(t, idx, axis=-1)


def rope_angles(pos, dim):
    inv_freq = ROPE_THETA ** (-jnp.arange(0, dim, 2, dtype=jnp.float32) / dim)
    return pos.astype(jnp.float32)[:, None] * inv_freq[None, :]


def apply_rope(x, ang):
    cos = jnp.cos(ang)[None, :, None, :]
    sin = jnp.sin(ang)[None, :, None, :]
    x1, x2 = jnp.split(x.astype(jnp.float32), 2, axis=-1)
    return jnp.concatenate([x1 * cos - x2 * sin, x1 * sin + x2 * cos], axis=-1).astype(x.dtype)


def apply_axial_rope(x, ang_row, ang_col):
    xr, xc = jnp.split(x, 2, axis=-1)
    return jnp.concatenate([apply_rope(xr, ang_row), apply_rope(xc, ang_col)], axis=-1)


def block_attention(q, k, v, scale):
    B, L, Hq, dk = q.shape
    Hkv, dv = v.shape[2], v.shape[3]
    G = Hq // Hkv
    S = L - N_META
    n_blocks = S // Q_BLOCK
    qg = q.reshape(B, L, Hkv, G, dk)

    def attend(qb):
        s = jnp.einsum('bqhgd,bkhd->bhgqk', qb, k, preferred_element_type=jnp.float32) * scale
        probs = jax.nn.softmax(s, axis=-1)
        return jnp.einsum('bhgqk,bkhd->bqhgd', probs.astype(v.dtype), v)

    o_meta = attend(qg[:, :N_META])
    q_blocks = qg[:, N_META:].reshape(B, n_blocks, Q_BLOCK, Hkv, G, dk).transpose(1, 0, 2, 3, 4, 5)
    o_real = lax.map(attend, q_blocks)
    o_real = o_real.transpose(1, 0, 2, 3, 4, 5).reshape(B, S, Hkv, G, dv)
    return jnp.concatenate([o_meta, o_real], axis=1).reshape(B, L, Hq * dv)


def gla_chunked(q, k, v, g):
    B, NC, C, H, dk = q.shape
    dv = v.shape[-1]
    b = jnp.cumsum(g, axis=2)
    b_last = b[:, :, -1:]
    q_e = q * jnp.exp(b)
    k_e = k * jnp.exp(-b)
    k_d = k * jnp.exp(b_last - b)
    att = jnp.einsum('bnihk,bnjhk->bnhij', q_e, k_e)
    mask = jnp.tril(jnp.ones((C, C), dtype=bool))
    att = jnp.where(mask, att, 0.0)
    o_intra = jnp.einsum('bnhij,bnjhv->bnihv', att, v)
    d_state = jnp.einsum('bnjhk,bnjhv->nbhkv', k_d, v)
    decay = jnp.exp(b_last[:, :, 0]).transpose(1, 0, 2, 3)

    def step(state, inp):
        dec, ds = inp
        return dec[..., None] * state + ds, state

    s0 = jnp.zeros((B, H, dk, dv), jnp.float32)
    _, s_prev = lax.scan(step, s0, (decay, d_state))
    o_inter = jnp.einsum('bnihk,nbhkv->bnihv', q_e, s_prev)
    return o_intra + o_inter


def gla_bidirectional(q, k, v, g_fw, g_bw):
    B, L, H, _ = q.shape
    pad = GLA_CHUNK - N_META
    Lp = L + pad
    nc = Lp // GLA_CHUNK

    def prep(t):
        return jnp.pad(t.astype(jnp.float32), ((0, 0), (pad, 0), (0, 0), (0, 0)))

    def chunks(t):
        return t.reshape(B, nc, GLA_CHUNK, H, t.shape[-1])

    def rev(t):
        return jnp.flip(t, axis=1)

    qp, kp, vp, gfp, gbp = prep(q), prep(k), prep(v), prep(g_fw), prep(g_bw)
    o_fw = gla_chunked(chunks(qp), chunks(kp), chunks(vp), chunks(gfp)).reshape(B, Lp, H, GLA_DV)
    o_bw = rev(gla_chunked(chunks(rev(qp)), chunks(rev(kp)), chunks(rev(vp)), chunks(rev(gbp))).reshape(B, Lp, H, GLA_DV))
    return (o_fw + o_bw)[:, pad:]


def token_mixer(h, i, p, ang_1d, ang_row, ang_col):
    B, L, _ = h.shape
    (c_q, c_kv, k_rope, gq, gk, gv, lq, lk, lv, lg_fw, lg_bw, l_gate) = split_cols(h @ p['w_in'][i], IN_SPLITS)

    q = (rms_norm(c_q, p['mla_q_norm'][i]) @ p['mla_w_uq'][i]).reshape(B, L, MLA_HEADS, MLA_NOPE + MLA_ROPE)
    q_nope, q_pe = jnp.split(q, [MLA_NOPE], axis=-1)
    kv = (rms_norm(c_kv, p['mla_kv_norm'][i]) @ p['mla_w_ukv'][i]).reshape(B, L, MLA_HEADS, MLA_NOPE + MLA_V)
    k_nope, v_mla = jnp.split(kv, [MLA_NOPE], axis=-1)
    k_pe = apply_rope(k_rope[:, :, None, :], ang_1d)
    q_mla = jnp.concatenate([q_nope, apply_rope(q_pe, ang_1d)], axis=-1)
    k_mla = jnp.concatenate([k_nope, jnp.broadcast_to(k_pe, (B, L, MLA_HEADS, MLA_ROPE))], axis=-1)
    o_mla = block_attention(q_mla, k_mla, v_mla, (MLA_NOPE + MLA_ROPE) ** -0.5)

    q_g = apply_axial_rope(rms_norm(gq.reshape(B, L, GQA_HEADS, GQA_HEAD_DIM), p['gqa_q_norm'][i]), ang_row, ang_col)
    k_g = apply_axial_rope(rms_norm(gk.reshape(B, L, GQA_KV_HEADS, GQA_HEAD_DIM), p['gqa_k_norm'][i]), ang_row, ang_col)
    v_g = gv.reshape(B, L, GQA_KV_HEADS, GQA_HEAD_DIM)
    o_gqa = block_attention(q_g, k_g, v_g, GQA_HEAD_DIM ** -0.5)

    q_l = lq.reshape(B, L, GLA_HEADS, GLA_DK) * (GLA_DK ** -0.5)
    k_l = lk.reshape(B, L, GLA_HEADS, GLA_DK)
    v_l = lv.reshape(B, L, GLA_HEADS, GLA_DV)
    g_fw = (jax.nn.log_sigmoid((lg_fw @ p['gla_gate_fw_w'][i] + p['gla_gate_fw_b'][i]).astype(jnp.float32))
            / GLA_GATE_NORM).reshape(B, L, GLA_HEADS, GLA_DK)
    g_bw = (jax.nn.log_sigmoid((lg_bw @ p['gla_gate_bw_w'][i] + p['gla_gate_bw_b'][i]).astype(jnp.float32))
            / GLA_GATE_NORM).reshape(B, L, GLA_HEADS, GLA_DK)
    o_l = gla_bidirectional(q_l, k_l, v_l, g_fw, g_bw)
    o_l = rms_norm(o_l, p['gla_out_norm'][i]) * jax.nn.silu(l_gate.reshape(B, L, GLA_HEADS, GLA_DV).astype(jnp.float32))
    o_gla = o_l.reshape(B, L, GLA_HEADS * GLA_DV).astype(h.dtype)

    return jnp.concatenate([o_mla, o_gqa, o_gla], axis=-1) @ p['w_out'][i]


def encoder(x, p):
    B, S, D = x.shape
    rows = S // GRID_W
    L = N_META + S
    meta = jnp.broadcast_to(p['meta_tokens'].astype(x.dtype)[None], (B, N_META, D))
    h = jnp.concatenate([meta, x], axis=1)
    ang_1d = rope_angles(jnp.arange(L, dtype=jnp.int32), MLA_ROPE)
    zeros = jnp.zeros((N_META,), jnp.int32)
    row_pos = jnp.concatenate([zeros, jnp.repeat(jnp.arange(rows, dtype=jnp.int32), GRID_W)])
    col_pos = jnp.concatenate([zeros, jnp.tile(jnp.arange(GRID_W, dtype=jnp.int32), rows)])
    ang_row = rope_angles(row_pos, GQA_HEAD_DIM // 2)
    ang_col = rope_angles(col_pos, GQA_HEAD_DIM // 2)
    for i in range(DEPTH):
        h = h + 0.5 * swiglu(rms_norm(h, p['ffn1_norm'][i]), p['ffn1_w_in'][i], p['ffn1_w_out'][i])
        h = h + token_mixer(rms_norm(h, p['mix_norm'][i]), i, p, ang_1d, ang_row, ang_col)
        h = h + 0.5 * swiglu(rms_norm(h, p['ffn2_norm'][i]), p['ffn2_w_in'][i], p['ffn2_w_out'][i])
    y = rms_norm(h, p['final_norm'])
    return y[:, N_META:]


def setup_inputs(seed: int = 0) -> dict:
    key = jax.random.key(seed)
    ks = jax.random.split(key, 24)

    def nrm(k, shape, scale):
        return jax.random.normal(k, shape, jnp.float32) * scale

    def gain(k, shape):
        return 1.0 + 0.02 * jax.random.normal(k, shape, jnp.float32)

    return {
        'x_prompt': nrm(ks[0], (BATCH, SEQ, D_MODEL), 1.0),
        'x_sample': nrm(ks[1], (DEC_BATCH, DEC_SEQ, D_MODEL), 1.0),
        'meta_tokens': nrm(ks[2], (N_META, D_MODEL), 1.0),
        'final_norm': gain(ks[3], (D_MODEL,)),
        'ffn1_norm': gain(ks[4], (DEPTH, D_MODEL)),
        'ffn1_w_in': nrm(ks[5], (DEPTH, D_MODEL, 2 * D_FF), D_MODEL ** -0.5),
        'ffn1_w_out': nrm(ks[6], (DEPTH, D_FF, D_MODEL), D_FF ** -0.5),
        'mix_norm': gain(ks[7], (DEPTH, D_MODEL)),
        'w_in': nrm(ks[8], (DEPTH, D_MODEL, IN_COLS), D_MODEL ** -0.5),
        'w_out': nrm(ks[9], (DEPTH, MIX_W, D_MODEL), MIX_W ** -0.5),
        'mla_q_norm': gain(ks[10], (DEPTH, MLA_Q_LORA)),
        'mla_w_uq': nrm(ks[11], (DEPTH, MLA_Q_LORA, MLA_HEADS * (MLA_NOPE + MLA_ROPE)), MLA_Q_LORA ** -0.5),
        'mla_kv_norm': gain(ks[12], (DEPTH, MLA_KV_LORA)),
        'mla_w_ukv': nrm(ks[13], (DEPTH, MLA_KV_LORA, MLA_HEADS * (MLA_NOPE + MLA_V)), MLA_KV_LORA ** -0.5),
        'gqa_q_norm': gain(ks[14], (DEPTH, GQA_HEAD_DIM)),
        'gqa_k_norm': gain(ks[15], (DEPTH, GQA_HEAD_DIM)),
        'gla_gate_fw_w': nrm(ks[16], (DEPTH, GLA_GATE_RANK, GLA_HEADS * GLA_DK), GLA_GATE_RANK ** -0.5),
        'gla_gate_fw_b': nrm(ks[17], (DEPTH, GLA_HEADS * GLA_DK), 0.1),
        'gla_gate_bw_w': nrm(ks[18], (DEPTH, GLA_GATE_RANK, GLA_HEADS * GLA_DK), GLA_GATE_RANK ** -0.5),
        'gla_gate_bw_b': nrm(ks[19], (DEPTH, GLA_HEADS * GLA_DK), 0.1),
        'gla_out_norm': gain(ks[20], (DEPTH, GLA_DV)),
        'ffn2_norm': gain(ks[21], (DEPTH, D_MODEL)),
        'ffn2_w_in': nrm(ks[22], (DEPTH, D_MODEL, 2 * D_FF), D_MODEL ** -0.5),
        'ffn2_w_out': nrm(ks[23], (DEPTH, D_FF, D_MODEL), D_FF ** -0.5),
    }


def reference(x_prompt, x_sample, meta_tokens, final_norm, ffn1_norm, ffn1_w_in, ffn1_w_out,
              mix_norm, w_in, w_out, mla_q_norm, mla_w_uq, mla_kv_norm, mla_w_ukv,
              gqa_q_norm, gqa_k_norm, gla_gate_fw_w, gla_gate_fw_b, gla_gate_bw_w, gla_gate_bw_b,
              gla_out_norm, ffn2_norm, ffn2_w_in, ffn2_w_out):
    p = {
        'meta_tokens': meta_tokens, 'final_norm': final_norm,
        'ffn1_norm': ffn1_norm, 'ffn1_w_in': ffn1_w_in, 'ffn1_w_out': ffn1_w_out,
        'mix_norm': mix_norm, 'w_in': w_in, 'w_out': w_out,
        'mla_q_norm': mla_q_norm, 'mla_w_uq': mla_w_uq, 'mla_kv_norm': mla_kv_norm, 'mla_w_ukv': mla_w_ukv,
        'gqa_q_norm': gqa_q_norm, 'gqa_k_norm': gqa_k_norm,
        'gla_gate_fw_w': gla_gate_fw_w, 'gla_gate_fw_b': gla_gate_fw_b,
        'gla_gate_bw_w': gla_gate_bw_w, 'gla_gate_bw_b': gla_gate_bw_b,
        'gla_out_norm': gla_out_norm,
        'ffn2_norm': ffn2_norm, 'ffn2_w_in': ffn2_w_in, 'ffn2_w_out': ffn2_w_out,
    }
    y_prompt = encoder(x_prompt, p)
    y_sample = encoder(x_sample, p)
    return (y_prompt, y_sample)
```

```python
import functools

import numpy as np
import jax
import jax.numpy as jnp
from jax import lax
from jax.experimental import pallas as pl
from jax.experimental.pallas import tpu as pltpu

F32 = jnp.float32
BF16 = jnp.bfloat16

D_MODEL = 1024
N_META = 16
GRID_W = 64
NORM_EPS = 1e-6
ROPE_THETA = 10000.0
D_FF = 2816
MLA_HEADS = 6
MLA_Q_LORA = 256
MLA_KV_LORA = 128
MLA_NOPE = 64
MLA_ROPE = 32
MLA_V = 64
GQA_HEADS = 6
GQA_KV_HEADS = 2
GQA_HEAD_DIM = 64
GLA_HEADS = 4
GLA_DK = 32
GLA_DV = 64
GLA_GATE_RANK = 16
GLA_GATE_NORM = 16.0
GLA_CHUNK = 64

LANES = 128
TM = 512
META_BLOCK = 128
HEAD_PAD = 128
VMEM_LIMIT = 56 * 1024 * 1024

MLA_SCALE = (MLA_NOPE + MLA_ROPE) ** -0.5
GQA_SCALE = GQA_HEAD_DIM ** -0.5
GLA_QSCALE = GLA_DK ** -0.5

C_CQ = 0
C_CKV = C_CQ + MLA_Q_LORA
C_KP = C_CKV + MLA_KV_LORA
C_KPR = C_KP + HEAD_PAD
C_GQ = C_KPR + HEAD_PAD
C_GQR = C_GQ + GQA_HEADS * HEAD_PAD
C_GK = C_GQR + GQA_HEADS * HEAD_PAD
C_GKR = C_GK + 128
C_LQ = C_GKR + 128
C_LK = C_LQ + 128
C_LV = C_LK + 128
C_LGATE = C_LV + 256
C_LG = C_LGATE + 256
C_END = C_LG + 128

NT_DIMS = (((1,), (1,)), ((), ()))
TN_DIMS = (((0,), (0,)), ((), ()))


def _dot(a, b):
    return jnp.dot(a, b, preferred_element_type=F32)


def _dot_nt(a, b):
    return lax.dot_general(a, b, NT_DIMS, preferred_element_type=F32)


def _dot_tn(a, b):
    return lax.dot_general(a, b, TN_DIMS, preferred_element_type=F32)


def _rms(x, w):
    return x * lax.rsqrt(jnp.mean(x * x, axis=-1, keepdims=True) + NORM_EPS) * w


def _params(sem):
    return pltpu.CompilerParams(dimension_semantics=sem, vmem_limit_bytes=VMEM_LIMIT)


def _const_spec(shape):
    nd = len(shape)
    return pl.BlockSpec(shape, lambda *_: (0,) * nd, pipeline_mode=pl.Buffered(1))


def _ffn_body(*refs, has_proj, has_final, n_split, tiles_p, tiles_s):
    it = iter(refs)
    h_ref = next(it)
    if has_proj:
        oa_ref, ob_ref, oc_ref, wa_ref, wb_ref, wc_ref = (next(it) for _ in range(6))
    nw_ref, wg_ref, wu_ref, wo_ref = (next(it) for _ in range(4))
    if has_final:
        fw_ref, yp_ref, ys_ref = next(it), next(it), next(it)
    else:
        out_ref = next(it)

    h = h_ref[...]
    if has_proj:
        h = (h + _dot(oa_ref[...], wa_ref[...]) + _dot(ob_ref[...], wb_ref[...])
             + _dot(oc_ref[...], wc_ref[...]))
    xn = _rms(h, nw_ref[...]).astype(BF16)
    tf = D_FF // n_split
    acc = jnp.zeros(h.shape, F32)
    for c in range(n_split):
        g = _dot(xn, wg_ref[:, c * tf:(c + 1) * tf])
        u = _dot(xn, wu_ref[:, c * tf:(c + 1) * tf])
        a = (g * jax.nn.sigmoid(g) * u).astype(BF16)
        acc = acc + _dot(a, wo_ref[c * tf:(c + 1) * tf, :])
    h2 = h + 0.5 * acc
    if has_final:
        y = _rms(h2, fw_ref[...])
        i = pl.program_id(0)

        @pl.when(i < tiles_p)
        def _():
            yp_ref[...] = y

        @pl.when(jnp.logical_and(i >= tiles_p, i < tiles_p + tiles_s))
        def _():
            ys_ref[...] = y
    else:
        out_ref[...] = h2


def _ffn_call(h, nw, wg, wu, wo, proj=None, final=None, lay=None):
    R = h.shape[0]
    n_tiles = R // TM
    row = lambda w: pl.BlockSpec((TM, w), lambda i: (i, 0))
    ins, specs = [h], [row(D_MODEL)]
    if proj is not None:
        oa, ob, oc, wa, wb, wc = proj
        ins += [oa, ob, oc, wa, wb, wc]
        specs += [row(oa.shape[1]), row(ob.shape[1]), row(oc.shape[1]),
                  _const_spec(wa.shape), _const_spec(wb.shape), _const_spec(wc.shape)]
    ins += [nw, wg, wu, wo]
    specs += [_const_spec(nw.shape), _const_spec(wg.shape), _const_spec(wu.shape), _const_spec(wo.shape)]
    tiles_p = tiles_s = 0
    if final is not None:
        tiles_p, tiles_s = lay["p"]["main_rows"] // TM, lay["s"]["main_rows"] // TM
        ins.append(final)
        specs.append(_const_spec(final.shape))
        out_shape = (jax.ShapeDtypeStruct((tiles_p * TM, D_MODEL), F32),
                     jax.ShapeDtypeStruct((tiles_s * TM, D_MODEL), F32))
        out_specs = (pl.BlockSpec((TM, D_MODEL), lambda i: (jnp.minimum(i, tiles_p - 1), 0)),
                     pl.BlockSpec((TM, D_MODEL), lambda i: (jnp.clip(i - tiles_p, 0, tiles_s - 1), 0)))
    else:
        out_shape = jax.ShapeDtypeStruct((R, D_MODEL), F32)
        out_specs = row(D_MODEL)
    body = functools.partial(_ffn_body, has_proj=proj is not None, has_final=final is not None,
                             n_split=2, tiles_p=tiles_p, tiles_s=tiles_s)
    return pl.pallas_call(
        body, grid=(n_tiles,), in_specs=specs, out_specs=out_specs, out_shape=out_shape,
        compiler_params=_params(("arbitrary",)),
        name="ffn" + ("_proj" if proj is not None else "") + ("_final" if final is not None else ""),
    )(*ins)


def _mixin_body(tab_ref, h_ref, nw_ref, wbig_ref, cm_ref, sm_ref, cg_ref, sg_ref,
                qn_ref, kvn_ref, wqa_ref, wqb_ref, wka_ref, wvt_ref,
                gqc_ref, gqs_ref, gkc_ref, gks_ref, wgvt_ref, wg_ref, bg_ref,
                qm_ref, km_ref, vmt_ref, qg_ref, kg_ref, vgt_ref,
                lq_ref, lk_ref, lv_ref, gfw_ref, gbw_ref, lgate_ref):
    del tab_ref
    hn = _rms(h_ref[...], nw_ref[...]).astype(BF16)

    def proj(a, width):
        return _dot(hn, wbig_ref[:, a:a + width])

    cm, sm = cm_ref[...], sm_ref[...]
    cg, sg = cg_ref[...], sg_ref[...]

    cqn = _rms(proj(C_CQ, MLA_Q_LORA), qn_ref[...]).astype(BF16)
    qa = _dot(cqn, wqa_ref[...])
    qb = _dot(cqn, wqb_ref[...])
    for hd in range(MLA_HEADS):
        blk = slice(hd * HEAD_PAD, (hd + 1) * HEAD_PAD)
        qm_ref[:, blk] = ((qa[:, blk] * cm + qb[:, blk] * sm) * MLA_SCALE).astype(BF16)

    kvn = _rms(proj(C_CKV, MLA_KV_LORA), kvn_ref[...]).astype(BF16)
    kpe = proj(C_KP, HEAD_PAD) * cm + proj(C_KPR, HEAD_PAD) * sm
    ka = _dot(kvn, wka_ref[...])
    for hd in range(MLA_HEADS):
        blk = slice(hd * HEAD_PAD, (hd + 1) * HEAD_PAD)
        km_ref[:, blk] = (ka[:, blk] + kpe).astype(BF16)
    vmt_ref[0] = _dot_nt(wvt_ref[...], kvn).astype(BF16)

    tqc, tqs = gqc_ref[...] * cg, gqs_ref[...] * sg
    for hd in range(GQA_HEADS):
        x = proj(C_GQ + hd * HEAD_PAD, HEAD_PAD)
        xr = proj(C_GQR + hd * HEAD_PAD, HEAD_PAD)
        n = lax.rsqrt(jnp.sum(x * x, axis=-1, keepdims=True) * (1.0 / GQA_HEAD_DIM) + NORM_EPS)
        qg_ref[:, hd * HEAD_PAD:(hd + 1) * HEAD_PAD] = (n * (x * tqc + xr * tqs) * GQA_SCALE).astype(BF16)
    x = proj(C_GK, 128)
    xr = proj(C_GKR, 128)
    lo = lax.broadcasted_iota(jnp.int32, x.shape, 1) < GQA_HEAD_DIM
    x2 = x * x
    ms0 = jnp.sum(jnp.where(lo, x2, 0.0), axis=-1, keepdims=True) * (1.0 / GQA_HEAD_DIM)
    ms1 = jnp.sum(jnp.where(lo, 0.0, x2), axis=-1, keepdims=True) * (1.0 / GQA_HEAD_DIM)
    n = jnp.where(lo, lax.rsqrt(ms0 + NORM_EPS), lax.rsqrt(ms1 + NORM_EPS))
    kg_ref[...] = (n * (x * (gkc_ref[...] * cg) + xr * (gks_ref[...] * sg))).astype(BF16)
    vgt_ref[0] = _dot_nt(wgvt_ref[...], hn).astype(BF16)

    lq_ref[...] = proj(C_LQ, 128) * GLA_QSCALE
    lk_ref[...] = proj(C_LK, 128)
    lv_ref[...] = proj(C_LV, 256)
    lgate_ref[...] = proj(C_LGATE, 256)
    gg = _dot(proj(C_LG, 128).astype(BF16), wg_ref[...]) + bg_ref[...]
    ls = (jnp.minimum(gg, 0.0) - jnp.log1p(jnp.exp(-jnp.abs(gg)))) * (1.0 / GLA_GATE_NORM)
    gfw_ref[...] = ls[:, :128]
    gbw_ref[...] = ls[:, 128:]


def _mixin_call(h, tile_tab, consts, tabs):
    R = h.shape[0]
    n_tiles = R // TM
    row = lambda w: pl.BlockSpec((TM, w), lambda i, t: (i, 0))
    tabspec = pl.BlockSpec((TM, LANES), lambda i, t: (t[i], 0))
    cs = lambda a: _const_spec(a.shape)
    col3 = lambda r: pl.BlockSpec((1, r, TM), lambda i, t: (i, 0, 0))
    ins = [h, consts["nw"], consts["wbig"], tabs["cm"], tabs["sm"], tabs["cg"], tabs["sg"],
           consts["qn"], consts["kvn"], consts["wqa"], consts["wqb"], consts["wka"], consts["wvt"],
           consts["gqc"], consts["gqs"], consts["gkc"], consts["gks"], consts["wgvt"],
           consts["wg"], consts["bg"]]
    specs = [row(D_MODEL), cs(ins[1]), cs(ins[2]), tabspec, tabspec, tabspec, tabspec] + [cs(a) for a in ins[7:]]
    W6 = MLA_HEADS * HEAD_PAD
    out_shape = (
        jax.ShapeDtypeStruct((R, W6), BF16), jax.ShapeDtypeStruct((R, W6), BF16),
        jax.ShapeDtypeStruct((n_tiles, MLA_HEADS * MLA_V, TM), BF16),
        jax.ShapeDtypeStruct((R, W6), BF16), jax.ShapeDtypeStruct((R, 128), BF16),
        jax.ShapeDtypeStruct((n_tiles, 128, TM), BF16),
        jax.ShapeDtypeStruct((R, 128), F32), jax.ShapeDtypeStruct((R, 128), F32),
        jax.ShapeDtypeStruct((R, 256), F32), jax.ShapeDtypeStruct((R, 128), F32),
        jax.ShapeDtypeStruct((R, 128), F32), jax.ShapeDtypeStruct((R, 256), F32),
    )
    out_specs = (row(W6), row(W6), col3(MLA_HEADS * MLA_V), row(W6), row(128), col3(128),
                 row(128), row(128), row(256), row(128), row(128), row(256))
    gs = pltpu.PrefetchScalarGridSpec(num_scalar_prefetch=1, grid=(n_tiles,), in_specs=specs, out_specs=out_specs)
    return pl.pallas_call(_mixin_body, grid_spec=gs, out_shape=out_shape,
                          compiler_params=_params(("arbitrary",)), name="mixer_in")(tile_tab, *ins)


def _attn_body(q_ref, k_ref, kmeta_ref, vt_ref, vtmeta_ref, o_ref, *scratch,
               gqa, tq, n_chunks, meta_q, b_axis, p_axis):
    b = pl.program_id(b_axis)
    p = pl.program_id(p_axis)
    slot = b % (META_BLOCK // N_META)
    rows = lax.broadcasted_iota(jnp.int32, (META_BLOCK, 1), 0)
    meta_valid = (rows // N_META) == slot
    halves = []
    for i in range(2):
        q = q_ref[:, i * HEAD_PAD:(i + 1) * HEAD_PAD]
        ks = slice(0, HEAD_PAD) if gqa else slice(i * HEAD_PAD, (i + 1) * HEAD_PAD)
        vs = slice(0, 2 * MLA_V) if gqa else slice(i * MLA_V, (i + 1) * MLA_V)
        vr = vs.stop - vs.start

        def update(carry, s_t, v_t):
            m, l, acc = carry
            m_new = jnp.maximum(m, jnp.max(s_t, axis=0, keepdims=True))
            p_t = jnp.exp(s_t - m_new)
            alpha = jnp.exp(m - m_new)
            l = alpha * l + jnp.sum(p_t, axis=0, keepdims=True)
            acc = alpha * acc + _dot(v_t, p_t.astype(BF16))
            return m_new, l, acc

        def step(c, carry, ks=ks, vs=vs, q=q):
            k = k_ref[pl.ds(pl.multiple_of(c * TM, TM), TM), ks]
            return update(carry, _dot_nt(k, q), vt_ref[c, vs, :])

        init = (jnp.full((1, tq), -jnp.inf, F32), jnp.zeros((1, tq), F32), jnp.zeros((vr, tq), F32))
        carry = lax.fori_loop(0, n_chunks, step, init)
        s_meta = jnp.where(meta_valid, _dot_nt(kmeta_ref[:, ks], q), -jnp.inf)
        _, l, acc = update(carry, s_meta, vtmeta_ref[0, vs, :])
        o_t = acc / l
        if gqa:
            group = (2 * p + i) // (GQA_HEADS // GQA_KV_HEADS)
            o_t = jnp.where(group == 0, o_t[:GQA_HEAD_DIM], o_t[GQA_HEAD_DIM:])
        halves.append(o_t)
    o = jnp.concatenate(halves, axis=0).T.astype(BF16)
    if meta_q:
        (stage,) = scratch

        @pl.when(slot == 0)
        def _():
            o_ref[...] = jnp.zeros(o_ref.shape, o_ref.dtype)

        stage[...] = o
        off = pl.multiple_of(slot * N_META, N_META)
        o_ref[pl.ds(off, N_META), :] = stage[pl.ds(off, N_META), :]
    else:
        o_ref[...] = o


def _attn_call(q, k, vt, g, *, gqa, meta_q, tq):
    B, S = g["B"], g["S"]
    n_chunks = S // TM
    kw = HEAD_PAD if gqa else 2 * HEAD_PAD
    main_blk = g["main_off"] // S
    meta_blk = g["meta_off"] // META_BLOCK
    per_blk = META_BLOCK // N_META
    pk = (lambda p: 0) if gqa else (lambda p: p)
    if meta_q:
        grid = (3, B)
        ax = lambda f: (lambda p, b: f(b, p, 0))
        tq = META_BLOCK
        q_map = lambda b, p, t: (meta_blk + b // per_blk, p)
        o_map = lambda b, p, t: (b // per_blk, p)
        out_rows = pl.cdiv(B * N_META, META_BLOCK) * META_BLOCK
        sem = ("arbitrary", "arbitrary")
        scratch = [pltpu.VMEM((META_BLOCK, HEAD_PAD), BF16)]
        b_axis, p_axis = 1, 0
    else:
        nq = S // tq
        grid = (B, 3, nq)
        ax = lambda f: f
        q_map = lambda b, p, t: ((g["main_off"] + b * S) // tq + t, p)
        o_map = lambda b, p, t: (b * nq + t, p)
        out_rows = B * S
        sem = ("arbitrary", "arbitrary", "arbitrary")
        scratch = []
        b_axis, p_axis = 0, 1
    in_specs = [
        pl.BlockSpec((tq, 2 * HEAD_PAD), ax(q_map)),
        pl.BlockSpec((S, kw), ax(lambda b, p, t: (main_blk + b, pk(p)))),
        pl.BlockSpec((META_BLOCK, kw), ax(lambda b, p, t: (meta_blk + b // per_blk, pk(p)))),
        pl.BlockSpec((n_chunks, 128, TM), ax(lambda b, p, t: (main_blk + b, pk(p), 0))),
        pl.BlockSpec((1, 128, META_BLOCK),
                     ax(lambda b, p, t: (g["meta_off"] // TM + (b * N_META) // TM, pk(p),
                                         ((b * N_META) % TM) // META_BLOCK))),
    ]
    body = functools.partial(_attn_body, gqa=gqa, tq=tq, n_chunks=n_chunks, meta_q=meta_q,
                             b_axis=b_axis, p_axis=p_axis)
    return pl.pallas_call(
        body, grid=grid, in_specs=in_specs,
        out_specs=pl.BlockSpec((tq, HEAD_PAD), ax(o_map)),
        out_shape=jax.ShapeDtypeStruct((out_rows, 3 * HEAD_PAD), BF16),
        scratch_shapes=scratch, compiler_params=_params(sem),
        name=("gqa" if gqa else "mla") + ("_metaq_" if meta_q else "_") + g["name"],
    )(q, k, k, vt, vt)


def _gla_consts():
    c = GLA_CHUNK
    r = lax.broadcasted_iota(jnp.int32, (c, c), 0)
    s = lax.broadcasted_iota(jnp.int32, (c, c), 1)
    r4 = lax.broadcasted_iota(jnp.int32, (GLA_HEADS * c, c), 0) % c
    s4 = lax.broadcasted_iota(jnp.int32, (GLA_HEADS * c, c), 1)
    kl = lax.broadcasted_iota(jnp.int32, (1, GLA_HEADS * GLA_DK), 1) // GLA_DK
    vl = lax.broadcasted_iota(jnp.int32, (1, GLA_HEADS * GLA_DV), 1) // GLA_DV
    kr = lax.broadcasted_iota(jnp.int32, (GLA_HEADS * GLA_DK, GLA_HEADS * GLA_DV), 0) // GLA_DK
    vc = lax.broadcasted_iota(jnp.int32, (GLA_HEADS * GLA_DK, GLA_HEADS * GLA_DV), 1) // GLA_DV
    return dict(
        tril=(s <= r).astype(BF16), triu=(s >= r).astype(BF16),
        mask4_fw=s4 <= r4, mask4_bw=s4 >= r4,
        kmask=[(kl == h).astype(F32) for h in range(GLA_HEADS)],
        vmask=[(vl == h).astype(F32) for h in range(GLA_HEADS)],
        blockdiag=(kr == vc).astype(F32),
        ones=jnp.ones((c, GLA_HEADS * GLA_DV), BF16),
    )


def _gla_chunk(q, k, v, g, state, cst, fwd):
    c = GLA_CHUNK
    tri = cst["tril"] if fwd else cst["triu"]
    g_hi = g.astype(BF16)
    g_lo = (g - g_hi.astype(F32)).astype(BF16)
    bcum = _dot(tri, g_hi) + _dot(tri, g_lo)
    btot = bcum[c - 1:c, :] if fwd else bcum[0:1, :]
    qe = q * jnp.exp(bcum)
    ke = k * jnp.exp(-bcum)
    kd = k * jnp.exp(btot - bcum)
    vb = v.astype(BF16)
    qstack = jnp.concatenate([qe * cst["kmask"][h] for h in range(GLA_HEADS)], axis=0).astype(BF16)
    att = _dot_nt(qstack, ke.astype(BF16))
    att = jnp.where(cst["mask4_fw"] if fwd else cst["mask4_bw"], att, 0.0)
    full = _dot(att.astype(BF16), vb)
    o = full[0:c] * cst["vmask"][0]
    for h in range(1, GLA_HEADS):
        o = o + full[h * c:(h + 1) * c] * cst["vmask"][h]
    o = o + _dot(qe.astype(BF16), state.astype(BF16))
    dstate = _dot_tn(kd.astype(BF16), vb) * cst["blockdiag"]
    decay = jnp.exp(_dot_tn(g_hi, cst["ones"]) + _dot_tn(g_lo, cst["ones"]))
    return o, decay * state + dstate


def _meta_chunk(ref):
    x = ref[...]
    return jnp.concatenate([jnp.zeros((GLA_CHUNK - N_META, x.shape[1]), x.dtype), x], axis=0)


def _gla_fw_body(q_ref, k_ref, v_ref, g_ref, qm_ref, km_ref, vm_ref, gm_ref, o_ref, om_ref, state):
    cst = _gla_consts()

    @pl.when(pl.program_id(1) == 0)
    def _():
        o, s = _gla_chunk(_meta_chunk(qm_ref), _meta_chunk(km_ref), _meta_chunk(vm_ref), _meta_chunk(gm_ref),
                          jnp.zeros(state.shape, F32), cst, True)
        om_ref[...] = o[GLA_CHUNK - N_META:]
        state[...] = s

    s = state[...]
    for c in range(TM // GLA_CHUNK):
        sl = slice(c * GLA_CHUNK, (c + 1) * GLA_CHUNK)
        o, s = _gla_chunk(q_ref[sl, :], k_ref[sl, :], v_ref[sl, :], g_ref[sl, :], s, cst, True)
        o_ref[sl, :] = o
    state[...] = s


def _gla_finish(o, gate, w, bd):
    x2 = o * o
    hi = x2.astype(BF16)
    lo = (x2 - hi.astype(F32)).astype(BF16)
    ms = _dot(hi, bd) + _dot(lo, bd)
    return (o * lax.rsqrt(ms + NORM_EPS) * w * (gate * jax.nn.sigmoid(gate))).astype(BF16)


def _gla_bw_body(q_ref, k_ref, v_ref, g_ref, ofw_ref, gate_ref, qm_ref, km_ref, vm_ref, gm_ref, ofwm_ref, gatem_ref,
                 w_ref, o_ref, om_ref, state):
    cst = _gla_consts()
    t = pl.program_id(1)
    r = lax.broadcasted_iota(jnp.int32, (GLA_HEADS * GLA_DV,) * 2, 0) // GLA_DV
    c_ = lax.broadcasted_iota(jnp.int32, (GLA_HEADS * GLA_DV,) * 2, 1) // GLA_DV
    bd = jnp.where(r == c_, 1.0 / GLA_DV, 0.0).astype(BF16)

    @pl.when(t == 0)
    def _():
        state[...] = jnp.zeros(state.shape, F32)

    s = state[...]
    for c in reversed(range(TM // GLA_CHUNK)):
        sl = slice(c * GLA_CHUNK, (c + 1) * GLA_CHUNK)
        o, s = _gla_chunk(q_ref[sl, :], k_ref[sl, :], v_ref[sl, :], g_ref[sl, :], s, cst, False)
        o_ref[sl, :] = _gla_finish(o + ofw_ref[sl, :], gate_ref[sl, :], w_ref[...], bd)
    state[...] = s

    @pl.when(t == pl.num_programs(1) - 1)
    def _():
        o, _ = _gla_chunk(_meta_chunk(qm_ref), _meta_chunk(km_ref), _meta_chunk(vm_ref), _meta_chunk(gm_ref),
                          s, cst, False)
        om_ref[...] = _gla_finish(o[GLA_CHUNK - N_META:] + ofwm_ref[...], gatem_ref[...], w_ref[...], bd)


def _gla_call(lq, lk, lv, gfw, gbw, lgate, wnorm, g):
    B, S = g["B"], g["S"]
    nt = S // TM
    main0 = g["main_off"] // TM
    meta0 = g["meta_off"] // N_META
    mrow = lambda w, rev: pl.BlockSpec(
        (TM, w), (lambda b, t: (main0 + b * nt + (nt - 1 - t), 0)) if rev else (lambda b, t: (main0 + b * nt + t, 0)))
    meta = lambda w: pl.BlockSpec((N_META, w), lambda b, t: (meta0 + b, 0))
    orow = lambda w, rev: pl.BlockSpec(
        (TM, w), (lambda b, t: (b * nt + (nt - 1 - t), 0)) if rev else (lambda b, t: (b * nt + t, 0)))
    ometa = lambda w: pl.BlockSpec((N_META, w), lambda b, t: (b, 0))
    sem = _params(("arbitrary", "arbitrary"))
    state = [pltpu.VMEM((GLA_HEADS * GLA_DK, GLA_HEADS * GLA_DV), F32)]
    ofw, ofw_meta = pl.pallas_call(
        _gla_fw_body, grid=(B, nt),
        in_specs=[mrow(128, False), mrow(128, False), mrow(256, False), mrow(128, False),
                  meta(128), meta(128), meta(256), meta(128)],
        out_specs=(orow(256, False), ometa(256)),
        out_shape=(jax.ShapeDtypeStruct((B * S, 256), F32), jax.ShapeDtypeStruct((B * N_META, 256), F32)),
        scratch_shapes=state, compiler_params=sem, name="gla_fw_" + g["name"],
    )(lq, lk, lv, gfw, lq, lk, lv, gfw)
    o, o_meta = pl.pallas_call(
        _gla_bw_body, grid=(B, nt),
        in_specs=[mrow(128, True), mrow(128, True), mrow(256, True), mrow(128, True), orow(256, True), mrow(256, True),
                  meta(128), meta(128), meta(256), meta(128), ometa(256), meta(256), _const_spec(wnorm.shape)],
        out_specs=(orow(256, True), ometa(256)),
        out_shape=(jax.ShapeDtypeStruct((B * S, 256), BF16), jax.ShapeDtypeStruct((B * N_META, 256), BF16)),
        scratch_shapes=state, compiler_params=sem, name="gla_bw_" + g["name"],
    )(lq, lk, lv, gbw, ofw, lgate, lq, lk, lv, gbw, ofw_meta, lgate, wnorm)
    return o, o_meta


def _rot_cols(w, n):
    s = w.shape
    w4 = w.reshape(s[:-1] + (s[-1] // (2 * n), 2, n))
    return jnp.stack([-w4[..., 1, :], w4[..., 0, :]], axis=-2).reshape(s)


def _swap_halves(w, n):
    w3 = w.reshape(-1, 2, n)
    return w3[:, ::-1, :].reshape(w.shape)


def _layout(bp, sp, bs, ss):
    up = lambda x: -(-x // TM) * TM
    lay = {"p": dict(name="p", B=bp, S=sp, main_off=0, main_rows=bp * sp),
           "s": dict(name="s", B=bs, S=ss, main_off=bp * sp, main_rows=bs * ss)}
    lay["p"]["meta_off"] = bp * sp + bs * ss
    lay["s"]["meta_off"] = lay["p"]["meta_off"] + up(bp * N_META)
    lay["R"] = lay["s"]["meta_off"] + up(bs * N_META)
    for g in (lay["p"], lay["s"]):
        assert g["S"] % TM == 0 and g["main_off"] % g["S"] == 0 and g["S"] % GRID_W == 0
    return lay


def _tile_table(lay):
    smax = max(lay["p"]["S"], lay["s"]["S"])
    tab = []
    for g in (lay["p"], lay["s"]):
        tab += [t % (g["S"] // TM) for t in range(g["main_rows"] // TM)]
    tab += [smax // TM] * ((lay["R"] - lay["p"]["meta_off"]) // TM)
    return jnp.asarray(np.asarray(tab, np.int32)), smax


def _rope_tables(smax):
    r = jnp.arange(smax, dtype=jnp.int32)
    meta = jnp.arange(TM, dtype=jnp.int32) % N_META
    zero = jnp.zeros((TM,), jnp.int32)
    pos = jnp.concatenate([r + N_META, meta]).astype(F32)
    row = jnp.concatenate([r // GRID_W, zero]).astype(F32)
    col = jnp.concatenate([r % GRID_W, zero]).astype(F32)
    inv_m = ROPE_THETA ** (-jnp.arange(0, MLA_ROPE, 2, dtype=F32) / MLA_ROPE)
    half = GQA_HEAD_DIM // 2
    inv_g = ROPE_THETA ** (-jnp.arange(0, half, 2, dtype=F32) / half)
    a1, ar, ac = pos[:, None] * inv_m, row[:, None] * inv_g, col[:, None] * inv_g
    n = pos.shape[0]
    one, zer = jnp.ones((n, MLA_NOPE), F32), jnp.zeros((n, MLA_NOPE), F32)
    pad = jnp.zeros((n, HEAD_PAD - MLA_NOPE - MLA_ROPE), F32)
    cm = jnp.concatenate([one, jnp.cos(a1), jnp.cos(a1), pad], axis=1)
    sm = jnp.concatenate([zer, jnp.sin(a1), jnp.sin(a1), pad], axis=1)
    cg = jnp.concatenate([jnp.cos(ar), jnp.cos(ar), jnp.cos(ac), jnp.cos(ac)] * 2, axis=1)
    sg = jnp.concatenate([jnp.sin(ar), jnp.sin(ar), jnp.sin(ac), jnp.sin(ac)] * 2, axis=1)
    return dict(cm=cm, sm=sm, cg=cg, sg=sg)


def _mixer_consts(i, mix_norm, w_in, mla_q_norm, mla_w_uq, mla_kv_norm, mla_w_ukv, gqa_q_norm, gqa_k_norm,
                  gate_fw_w, gate_fw_b, gate_bw_w, gate_bw_b):
    w = w_in[i]
    z = lambda n: jnp.zeros((D_MODEL, n), F32)
    o = np.cumsum([0, MLA_Q_LORA, MLA_KV_LORA, MLA_ROPE, 384, 128, 128, 128, 128, 256, 16, 16, 256])
    seg = lambda j: w[:, o[j]:o[j + 1]]
    w_kr, w_gq, w_gk = seg(2), seg(3), seg(4)
    w_gqr = _rot_cols(w_gq, 16)
    tail = HEAD_PAD - MLA_NOPE - MLA_ROPE

    def gq_blocks(m):
        out = []
        for h in range(GQA_HEADS):
            blk = m[:, h * GQA_HEAD_DIM:(h + 1) * GQA_HEAD_DIM]
            out += [blk, z(GQA_HEAD_DIM)] if h < GQA_HEADS // GQA_KV_HEADS else [z(GQA_HEAD_DIM), blk]
        return out

    wbig = jnp.concatenate(
        [seg(0), seg(1), z(MLA_NOPE), w_kr, z(tail), z(MLA_NOPE), _rot_cols(w_kr, 16), z(tail)]
        + gq_blocks(w_gq) + gq_blocks(w_gqr)
        + [w_gk, _rot_cols(w_gk, 16), seg(6), seg(7), seg(8), seg(11), seg(9), seg(10), z(128 - 2 * GLA_GATE_RANK)],
        axis=1).astype(BF16)
    assert wbig.shape[1] == C_END

    uq = mla_w_uq[i].reshape(MLA_Q_LORA, MLA_HEADS, MLA_NOPE + MLA_ROPE)
    zq = jnp.zeros((MLA_Q_LORA, MLA_HEADS, tail), F32)
    wqa = jnp.concatenate([uq, zq], axis=2).reshape(MLA_Q_LORA, -1).astype(BF16)
    wqb = jnp.concatenate([jnp.zeros((MLA_Q_LORA, MLA_HEADS, MLA_NOPE), F32),
                           _rot_cols(uq[:, :, MLA_NOPE:], 16), zq], axis=2).reshape(MLA_Q_LORA, -1).astype(BF16)
    ukv = mla_w_ukv[i].reshape(MLA_KV_LORA, MLA_HEADS, MLA_NOPE + MLA_V)
    wka = jnp.concatenate([ukv[:, :, :MLA_NOPE], jnp.zeros((MLA_KV_LORA, MLA_HEADS, HEAD_PAD - MLA_NOPE), F32)],
                          axis=2).reshape(MLA_KV_LORA, -1).astype(BF16)
    wvt = ukv[:, :, MLA_NOPE:].reshape(MLA_KV_LORA, -1).T.astype(BF16)
    wgvt = seg(5).T.astype(BF16)

    gq, gk = gqa_q_norm[i], gqa_k_norm[i]
    gqs, gks = _swap_halves(gq, 16), _swap_halves(gk, 16)
    two = lambda v: jnp.concatenate([v, v])[None, :]
    wg = jnp.zeros((128, 256), F32)
    wg = wg.at[:GLA_GATE_RANK, :128].set(gate_fw_w[i]).at[GLA_GATE_RANK:2 * GLA_GATE_RANK, 128:].set(gate_bw_w[i])
    return dict(
        nw=mix_norm[i][None, :], wbig=wbig, qn=mla_q_norm[i][None, :], kvn=mla_kv_norm[i][None, :],
        wqa=wqa, wqb=wqb, wka=wka, wvt=wvt, gqc=two(gq), gqs=two(gqs), gkc=two(gk), gks=two(gks),
        wgvt=wgvt, wg=wg.astype(BF16), bg=jnp.concatenate([gate_fw_b[i], gate_bw_b[i]])[None, :])


def _assemble(lay, parts):
    pieces = [parts["p"][0], parts["s"][0]]
    for name, nxt in (("p", lay["s"]["meta_off"]), ("s", lay["R"])):
        m = parts[name][1][:lay[name]["B"] * N_META]
        pad = nxt - lay[name]["meta_off"] - m.shape[0]
        pieces += [m, jnp.zeros((pad, m.shape[1]), m.dtype)]
    return jnp.concatenate(pieces, axis=0)


def kernel(x_prompt, x_sample, meta_tokens, final_norm, ffn1_norm, ffn1_w_in, ffn1_w_out, mix_norm, w_in, w_out,
           mla_q_norm, mla_w_uq, mla_kv_norm, mla_w_ukv, gqa_q_norm, gqa_k_norm, gla_gate_fw_w, gla_gate_fw_b,
           gla_gate_bw_w, gla_gate_bw_b, gla_out_norm, ffn2_norm, ffn2_w_in, ffn2_w_out):
    bp, sp, _ = x_prompt.shape
    bs, ss, _ = x_sample.shape
    depth = w_in.shape[0]
    lay = _layout(bp, sp, bs, ss)
    tile_tab, smax = _tile_table(lay)
    tabs = _rope_tables(smax)
    h = _assemble(lay, {"p": (x_prompt.reshape(-1, D_MODEL), jnp.tile(meta_tokens, (bp, 1))),
                        "s": (x_sample.reshape(-1, D_MODEL), jnp.tile(meta_tokens, (bs, 1)))})
    wnorm = jnp.tile(gla_out_norm, (1, GLA_HEADS))
    n_mla = MLA_HEADS * MLA_V
    tq = {"p": min(TM, sp), "s": min(TM, ss)}

    def ffn_w(w_i, w_o):
        return w_i[:, :D_FF].astype(BF16), w_i[:, D_FF:].astype(BF16), w_o.astype(BF16)

    y = None
    for i in range(depth):
        h = _ffn_call(h, ffn1_norm[i][None, :], *ffn_w(ffn1_w_in[i], ffn1_w_out[i]))
        consts = _mixer_consts(i, mix_norm, w_in, mla_q_norm, mla_w_uq, mla_kv_norm, mla_w_ukv, gqa_q_norm,
                               gqa_k_norm, gla_gate_fw_w, gla_gate_fw_b, gla_gate_bw_w, gla_gate_bw_b)
        qm, km, vmt, qg, kg, vgt, lq, lk, lv, gfw, gbw, lgate = _mixin_call(h, tile_tab, consts, tabs)
        o_mla, o_gqa, o_gla = {}, {}, {}
        for name in ("p", "s"):
            g = lay[name]
            o_mla[name] = (_attn_call(qm, km, vmt, g, gqa=False, meta_q=False, tq=tq[name]),
                           _attn_call(qm, km, vmt, g, gqa=False, meta_q=True, tq=None))
            o_gqa[name] = (_attn_call(qg, kg, vgt, g, gqa=True, meta_q=False, tq=tq[name]),
                           _attn_call(qg, kg, vgt, g, gqa=True, meta_q=True, tq=None))
            o_gla[name] = _gla_call(lq, lk, lv, gfw, gbw, lgate, wnorm[i][None, :], g)
        wo = w_out[i].astype(BF16)
        proj = (_assemble(lay, o_mla), _assemble(lay, o_gqa), _assemble(lay, o_gla),
                wo[:n_mla], wo[n_mla:2 * n_mla], wo[2 * n_mla:])
        last = i == depth - 1
        out = _ffn_call(h, ffn2_norm[i][None, :], *ffn_w(ffn2_w_in[i], ffn2_w_out[i]), proj=proj,
                        final=final_norm[None, :] if last else None, lay=lay)
        if last:
            y = out
        else:
            h = out
    return y[0].reshape(bp, sp, D_MODEL), y[1].reshape(bs, ss, D_MODEL)
```

```python
import functools

import numpy as np
import jax
import jax.numpy as jnp
from jax import lax
from jax.experimental import pallas as pl
from jax.experimental.pallas import tpu as pltpu

F32 = jnp.float32
BF16 = jnp.bfloat16

D_MODEL = 1024
N_META = 16
GRID_W = 64
NORM_EPS = 1e-6
ROPE_THETA = 10000.0
D_FF = 2816
MLA_HEADS = 6
MLA_Q_LORA = 256
MLA_KV_LORA = 128
MLA_NOPE = 64
MLA_ROPE = 32
MLA_V = 64
GQA_HEADS = 6
GQA_KV_HEADS = 2
GQA_HEAD_DIM = 64
GLA_HEADS = 4
GLA_DK = 32
GLA_DV = 64
GLA_GATE_RANK = 16
GLA_GATE_NORM = 16.0
GLA_CHUNK = 64

LANES = 128
TM = 512
META_BLOCK = 128
HEAD_PAD = 128
VMEM_LIMIT = 56 * 1024 * 1024
ATTN_UNROLL = 2
ATTN_STATIC_CHUNKS = 4

LOG2E = 1.4426950408889634
MLA_SCALE = (MLA_NOPE + MLA_ROPE) ** -0.5 * LOG2E
GQA_SCALE = GQA_HEAD_DIM ** -0.5 * LOG2E
SUM_ROWS = 16
VROWS_MLA = SUM_ROWS + MLA_V
VROWS_GQA = SUM_ROWS + GQA_KV_HEADS * GQA_HEAD_DIM
GLA_QSCALE = GLA_DK ** -0.5

C_CQ = 0
C_CKV = C_CQ + MLA_Q_LORA
C_KP = C_CKV + MLA_KV_LORA
C_KPR = C_KP + HEAD_PAD
C_GQ = C_KPR + HEAD_PAD
C_GQR = C_GQ + GQA_HEADS * HEAD_PAD
C_GK = C_GQR + GQA_HEADS * HEAD_PAD
C_GKR = C_GK + 128
C_LQ = C_GKR + 128
C_LK = C_LQ + 128
C_LV = C_LK + 128
C_LGATE = C_LV + 256
C_LG = C_LGATE + 256
C_END = C_LG + 128

NT_DIMS = (((1,), (1,)), ((), ()))
TN_DIMS = (((0,), (0,)), ((), ()))


def _dot(a, b):
    return jnp.dot(a, b, preferred_element_type=F32)


def _dot_nt(a, b):
    return lax.dot_general(a, b, NT_DIMS, preferred_element_type=F32)


def _dot_tn(a, b):
    return lax.dot_general(a, b, TN_DIMS, preferred_element_type=F32)


def _rms(x, w):
    return x * lax.rsqrt(jnp.mean(x * x, axis=-1, keepdims=True) + NORM_EPS) * w


def _params(sem):
    return pltpu.CompilerParams(dimension_semantics=sem, vmem_limit_bytes=VMEM_LIMIT)


def _const_spec(shape):
    nd = len(shape)
    return pl.BlockSpec(shape, lambda *_: (0,) * nd, pipeline_mode=pl.Buffered(1))


def _ffn_body(*refs, has_proj, has_final, n_split, tiles_p, tiles_s):
    it = iter(refs)
    h_ref = next(it)
    if has_proj:
        oa_ref, ob_ref, oc_ref, wa_ref, wb_ref, wc_ref = (next(it) for _ in range(6))
    nw_ref, wg_ref, wu_ref, wo_ref = (next(it) for _ in range(4))
    if has_final:
        fw_ref, yp_ref, ys_ref = next(it), next(it), next(it)
    else:
        out_ref = next(it)

    h = h_ref[...]
    if has_proj:
        h = (h + _dot(oa_ref[...], wa_ref[...]) + _dot(ob_ref[...], wb_ref[...])
             + _dot(oc_ref[...], wc_ref[...]))
    xn = _rms(h, nw_ref[...]).astype(BF16)
    tf = D_FF // n_split
    acc = jnp.zeros(h.shape, F32)
    for c in range(n_split):
        g = _dot(xn, wg_ref[:, c * tf:(c + 1) * tf])
        u = _dot(xn, wu_ref[:, c * tf:(c + 1) * tf])
        a = (g * jax.nn.sigmoid(g) * u).astype(BF16)
        acc = acc + _dot(a, wo_ref[c * tf:(c + 1) * tf, :])
    h2 = h + 0.5 * acc
    if has_final:
        y = _rms(h2, fw_ref[...])
        i = pl.program_id(0)

        @pl.when(i < tiles_p)
        def _():
            yp_ref[...] = y

        @pl.when(jnp.logical_and(i >= tiles_p, i < tiles_p + tiles_s))
        def _():
            ys_ref[...] = y
    else:
        out_ref[...] = h2


def _ffn_call(h, nw, wg, wu, wo, proj=None, final=None, lay=None):
    R = h.shape[0]
    n_tiles = R // TM
    row = lambda w: pl.BlockSpec((TM, w), lambda i: (i, 0))
    ins, specs = [h], [row(D_MODEL)]
    if proj is not None:
        oa, ob, oc, wa, wb, wc = proj
        ins += [oa, ob, oc, wa, wb, wc]
        specs += [row(oa.shape[1]), row(ob.shape[1]), row(oc.shape[1]),
                  _const_spec(wa.shape), _const_spec(wb.shape), _const_spec(wc.shape)]
    ins += [nw, wg, wu, wo]
    specs += [_const_spec(nw.shape), _const_spec(wg.shape), _const_spec(wu.shape), _const_spec(wo.shape)]
    tiles_p = tiles_s = 0
    if final is not None:
        tiles_p, tiles_s = lay["p"]["main_rows"] // TM, lay["s"]["main_rows"] // TM
        ins.append(final)
        specs.append(_const_spec(final.shape))
        out_shape = (jax.ShapeDtypeStruct((tiles_p * TM, D_MODEL), F32),
                     jax.ShapeDtypeStruct((tiles_s * TM, D_MODEL), F32))
        out_specs = (pl.BlockSpec((TM, D_MODEL), lambda i: (jnp.minimum(i, tiles_p - 1), 0)),
                     pl.BlockSpec((TM, D_MODEL), lambda i: (jnp.clip(i - tiles_p, 0, tiles_s - 1), 0)))
    else:
        out_shape = jax.ShapeDtypeStruct((R, D_MODEL), F32)
        out_specs = row(D_MODEL)
    body = functools.partial(_ffn_body, has_proj=proj is not None, has_final=final is not None,
                             n_split=2, tiles_p=tiles_p, tiles_s=tiles_s)
    return pl.pallas_call(
        body, grid=(n_tiles,), in_specs=specs, out_specs=out_specs, out_shape=out_shape,
        compiler_params=_params(("arbitrary",)),
        name="ffn" + ("_proj" if proj is not None else "") + ("_final" if final is not None else ""),
    )(*ins)


def _mixin_body(tab_ref, h_ref, nw_ref, wbig_ref, cm_ref, sm_ref, cg_ref, sg_ref,
                qn_ref, kvn_ref, wqa_ref, wqb_ref, wka_ref, wvt_ref,
                gqc_ref, gqs_ref, gkc_ref, gks_ref, wgvt_ref, wg_ref, bg_ref,
                qm_ref, km_ref, vmt_ref, qg_ref, kg_ref, vgt_ref,
                lq_ref, lk_ref, lv_ref, gfw_ref, gbw_ref, lgate_ref):
    del tab_ref
    hn = _rms(h_ref[...], nw_ref[...]).astype(BF16)

    def proj(a, width):
        return _dot(hn, wbig_ref[:, a:a + width])

    cm, sm = cm_ref[...], sm_ref[...]
    cg, sg = cg_ref[...], sg_ref[...]

    cqn = _rms(proj(C_CQ, MLA_Q_LORA), qn_ref[...]).astype(BF16)
    qa = _dot(cqn, wqa_ref[...])
    qb = _dot(cqn, wqb_ref[...])
    for hd in range(MLA_HEADS):
        blk = slice(hd * HEAD_PAD, (hd + 1) * HEAD_PAD)
        qm_ref[:, blk] = ((qa[:, blk] * cm + qb[:, blk] * sm) * MLA_SCALE).astype(BF16)

    kvn = _rms(proj(C_CKV, MLA_KV_LORA), kvn_ref[...]).astype(BF16)
    kpe = proj(C_KP, HEAD_PAD) * cm + proj(C_KPR, HEAD_PAD) * sm
    ka = _dot(kvn, wka_ref[...])
    for hd in range(MLA_HEADS):
        blk = slice(hd * HEAD_PAD, (hd + 1) * HEAD_PAD)
        km_ref[:, blk] = (ka[:, blk] + kpe).astype(BF16)
    ones = jnp.ones((SUM_ROWS, hn.shape[0]), F32)
    vt = _dot_nt(wvt_ref[...], kvn)
    pieces = []
    for hd in range(MLA_HEADS):
        pieces += [ones, vt[hd * MLA_V:(hd + 1) * MLA_V]]
    vmt_ref[0] = jnp.concatenate(pieces, axis=0).astype(BF16)

    tqc, tqs = gqc_ref[...] * cg, gqs_ref[...] * sg
    for hd in range(GQA_HEADS):
        x = proj(C_GQ + hd * HEAD_PAD, HEAD_PAD)
        xr = proj(C_GQR + hd * HEAD_PAD, HEAD_PAD)
        n = lax.rsqrt(jnp.sum(x * x, axis=-1, keepdims=True) * (1.0 / GQA_HEAD_DIM) + NORM_EPS)
        qg_ref[:, hd * HEAD_PAD:(hd + 1) * HEAD_PAD] = (n * (x * tqc + xr * tqs) * GQA_SCALE).astype(BF16)
    x = proj(C_GK, 128)
    xr = proj(C_GKR, 128)
    lo = lax.broadcasted_iota(jnp.int32, x.shape, 1) < GQA_HEAD_DIM
    x2 = x * x
    ms0 = jnp.sum(jnp.where(lo, x2, 0.0), axis=-1, keepdims=True) * (1.0 / GQA_HEAD_DIM)
    ms1 = jnp.sum(jnp.where(lo, 0.0, x2), axis=-1, keepdims=True) * (1.0 / GQA_HEAD_DIM)
    n = jnp.where(lo, lax.rsqrt(ms0 + NORM_EPS), lax.rsqrt(ms1 + NORM_EPS))
    kg_ref[...] = (n * (x * (gkc_ref[...] * cg) + xr * (gks_ref[...] * sg))).astype(BF16)
    vgt_ref[0] = jnp.concatenate([ones, _dot_nt(wgvt_ref[...], hn)], axis=0).astype(BF16)

    lq_ref[...] = proj(C_LQ, 128) * GLA_QSCALE
    lk_ref[...] = proj(C_LK, 128)
    lv_ref[...] = proj(C_LV, 256)
    lgate_ref[...] = proj(C_LGATE, 256)
    gg = _dot(proj(C_LG, 128).astype(BF16), wg_ref[...]) + bg_ref[...]
    ls = (jnp.minimum(gg, 0.0) - jnp.log1p(jnp.exp(-jnp.abs(gg)))) * (1.0 / GLA_GATE_NORM)
    gfw_ref[...] = ls[:, :128]
    gbw_ref[...] = ls[:, 128:]


def _mixin_call(h, tile_tab, consts, tabs):
    R = h.shape[0]
    n_tiles = R // TM
    row = lambda w: pl.BlockSpec((TM, w), lambda i, t: (i, 0))
    tabspec = pl.BlockSpec((TM, LANES), lambda i, t: (t[i], 0))
    cs = lambda a: _const_spec(a.shape)
    col3 = lambda r: pl.BlockSpec((1, r, TM), lambda i, t: (i, 0, 0))
    ins = [h, consts["nw"], consts["wbig"], tabs["cm"], tabs["sm"], tabs["cg"], tabs["sg"],
           consts["qn"], consts["kvn"], consts["wqa"], consts["wqb"], consts["wka"], consts["wvt"],
           consts["gqc"], consts["gqs"], consts["gkc"], consts["gks"], consts["wgvt"],
           consts["wg"], consts["bg"]]
    specs = [row(D_MODEL), cs(ins[1]), cs(ins[2]), tabspec, tabspec, tabspec, tabspec] + [cs(a) for a in ins[7:]]
    W6 = MLA_HEADS * HEAD_PAD
    out_shape = (
        jax.ShapeDtypeStruct((R, W6), BF16), jax.ShapeDtypeStruct((R, W6), BF16),
        jax.ShapeDtypeStruct((n_tiles, MLA_HEADS * VROWS_MLA, TM), BF16),
        jax.ShapeDtypeStruct((R, W6), BF16), jax.ShapeDtypeStruct((R, 128), BF16),
        jax.ShapeDtypeStruct((n_tiles, VROWS_GQA, TM), BF16),
        jax.ShapeDtypeStruct((R, 128), F32), jax.ShapeDtypeStruct((R, 128), F32),
        jax.ShapeDtypeStruct((R, 256), F32), jax.ShapeDtypeStruct((R, 128), F32),
        jax.ShapeDtypeStruct((R, 128), F32), jax.ShapeDtypeStruct((R, 256), F32),
    )
    out_specs = (row(W6), row(W6), col3(MLA_HEADS * VROWS_MLA), row(W6), row(128), col3(VROWS_GQA),
                 row(128), row(128), row(256), row(128), row(128), row(256))
    gs = pltpu.PrefetchScalarGridSpec(num_scalar_prefetch=1, grid=(n_tiles,), in_specs=specs, out_specs=out_specs)
    return pl.pallas_call(_mixin_body, grid_spec=gs, out_shape=out_shape,
                          compiler_params=_params(("arbitrary",)), name="mixer_in")(tile_tab, *ins)


def _attn_body(q_ref, k_ref, kmeta_ref, vt_ref, vtmeta_ref, o_ref, *scratch,
               gqa, tq, n_chunks, meta_q, b_axis, p_axis, unroll):
    b = pl.program_id(b_axis)
    p = pl.program_id(p_axis)
    slot = b % (META_BLOCK // N_META)
    rows = lax.broadcasted_iota(jnp.int32, (META_BLOCK, 1), 0)
    meta_valid = (rows // N_META) == slot
    halves = []
    for i in range(2):
        q = q_ref[:, i * HEAD_PAD:(i + 1) * HEAD_PAD]
        ks = slice(0, HEAD_PAD) if gqa else slice(i * HEAD_PAD, (i + 1) * HEAD_PAD)
        vs = slice(0, VROWS_GQA) if gqa else slice(i * VROWS_MLA, (i + 1) * VROWS_MLA)
        vr = vs.stop - vs.start

        def update(carry, s_t, v_t):
            m, acc = carry
            m_new = jnp.maximum(m, jnp.max(s_t, axis=0, keepdims=True))
            p_t = jnp.exp2(s_t - m_new).astype(BF16)
            return m_new, jnp.exp2(m - m_new) * acc + _dot(v_t, p_t)

        def scores(c, ks=ks, q=q):
            return _dot_nt(k_ref[pl.ds(pl.multiple_of(c * TM, TM), TM), ks], q)

        def meta_scores(ks=ks, q=q):
            return jnp.where(meta_valid, _dot_nt(kmeta_ref[:, ks], q), -jnp.inf)

        carry = (jnp.full((1, tq), -jnp.inf, F32), jnp.zeros((vr, tq), F32))
        if n_chunks <= ATTN_STATIC_CHUNKS:
            s_cur = scores(0)
            for c in range(n_chunks):
                s_next = scores(c + 1) if c + 1 < n_chunks else meta_scores()
                carry = update(carry, s_cur, vt_ref[c, vs, :])
                s_cur = s_next
            s_meta = s_cur
        else:
            s_buf = scratch[-1]
            s_buf[0] = scores(0)

            def group(j, st, vs=vs):
                for u in range(unroll):
                    c = j * unroll + u
                    s_buf[(u + 1) % 2] = scores(jnp.minimum(c + 1, n_chunks - 1))
                    st = update(st, s_buf[u % 2], vt_ref[c, vs, :])
                return st

            carry = lax.fori_loop(0, n_chunks // unroll, group, carry)
            s_meta = meta_scores()
        _, acc = update(carry, s_meta, vtmeta_ref[0, vs, :])
        o_t = acc[SUM_ROWS:] / acc[0:1]
        if gqa:
            group = (2 * p + i) // (GQA_HEADS // GQA_KV_HEADS)
            o_t = jnp.where(group == 0, o_t[:GQA_HEAD_DIM], o_t[GQA_HEAD_DIM:])
        halves.append(o_t)
    o = jnp.concatenate(halves, axis=0).T.astype(BF16)
    if meta_q:
        stage = scratch[0]

        @pl.when(slot == 0)
        def _():
            o_ref[...] = jnp.zeros(o_ref.shape, o_ref.dtype)

        stage[...] = o
        off = pl.multiple_of(slot * N_META, N_META)
        o_ref[pl.ds(off, N_META), :] = stage[pl.ds(off, N_META), :]
    else:
        o_ref[...] = o


def _attn_call(q, k, vt, g, *, gqa, meta_q, tq):
    B, S = g["B"], g["S"]
    n_chunks = S // TM
    kw = HEAD_PAD if gqa else 2 * HEAD_PAD
    vrows = VROWS_GQA if gqa else 2 * VROWS_MLA
    main_blk = g["main_off"] // S
    meta_blk = g["meta_off"] // META_BLOCK
    per_blk = META_BLOCK // N_META
    pk = (lambda p: 0) if gqa else (lambda p: p)
    if meta_q:
        grid = (3, B)
        ax = lambda f: (lambda p, b: f(b, p, 0))
        tq = META_BLOCK
        q_map = lambda b, p, t: (meta_blk + b // per_blk, p)
        o_map = lambda b, p, t: (b // per_blk, p)
        out_rows = pl.cdiv(B * N_META, META_BLOCK) * META_BLOCK
        sem = ("arbitrary", "arbitrary")
        scratch = [pltpu.VMEM((META_BLOCK, HEAD_PAD), BF16)]
        b_axis, p_axis = 1, 0
    else:
        nq = S // tq
        grid = (B, 3, nq)
        ax = lambda f: f
        q_map = lambda b, p, t: ((g["main_off"] + b * S) // tq + t, p)
        o_map = lambda b, p, t: (b * nq + t, p)
        out_rows = B * S
        sem = ("arbitrary", "arbitrary", "arbitrary")
        scratch = []
        b_axis, p_axis = 0, 1
    in_specs = [
        pl.BlockSpec((tq, 2 * HEAD_PAD), ax(q_map)),
        pl.BlockSpec((S, kw), ax(lambda b, p, t: (main_blk + b, pk(p)))),
        pl.BlockSpec((META_BLOCK, kw), ax(lambda b, p, t: (meta_blk + b // per_blk, pk(p)))),
        pl.BlockSpec((n_chunks, vrows, TM), ax(lambda b, p, t: (main_blk + b, pk(p), 0))),
        pl.BlockSpec((1, vrows, META_BLOCK),
                     ax(lambda b, p, t: (g["meta_off"] // TM + (b * N_META) // TM, pk(p),
                                         ((b * N_META) % TM) // META_BLOCK))),
    ]
    if n_chunks > ATTN_STATIC_CHUNKS:
        assert ATTN_UNROLL % 2 == 0 and n_chunks % ATTN_UNROLL == 0
        scratch = scratch + [pltpu.VMEM((2, TM, tq), F32)]
    body = functools.partial(_attn_body, gqa=gqa, tq=tq, n_chunks=n_chunks, meta_q=meta_q,
                             b_axis=b_axis, p_axis=p_axis, unroll=ATTN_UNROLL)
    return pl.pallas_call(
        body, grid=grid, in_specs=in_specs,
        out_specs=pl.BlockSpec((tq, HEAD_PAD), ax(o_map)),
        out_shape=jax.ShapeDtypeStruct((out_rows, 3 * HEAD_PAD), BF16),
        scratch_shapes=scratch, compiler_params=_params(sem),
        name=("gqa" if gqa else "mla") + ("_metaq_" if meta_q else "_") + g["name"],
    )(q, k, k, vt, vt)


def _gla_consts():
    c = GLA_CHUNK
    r = lax.broadcasted_iota(jnp.int32, (c, c), 0)
    s = lax.broadcasted_iota(jnp.int32, (c, c), 1)
    r4 = lax.broadcasted_iota(jnp.int32, (GLA_HEADS * c, c), 0) % c
    s4 = lax.broadcasted_iota(jnp.int32, (GLA_HEADS * c, c), 1)
    kl = lax.broadcasted_iota(jnp.int32, (1, GLA_HEADS * GLA_DK), 1) // GLA_DK
    vl = lax.broadcasted_iota(jnp.int32, (1, GLA_HEADS * GLA_DV), 1) // GLA_DV
    kr = lax.broadcasted_iota(jnp.int32, (GLA_HEADS * GLA_DK, GLA_HEADS * GLA_DV), 0) // GLA_DK
    vc = lax.broadcasted_iota(jnp.int32, (GLA_HEADS * GLA_DK, GLA_HEADS * GLA_DV), 1) // GLA_DV
    return dict(
        tril=(s <= r).astype(BF16), triu=(s >= r).astype(BF16),
        mask4_fw=s4 <= r4, mask4_bw=s4 >= r4,
        kmask=[(kl == h).astype(F32) for h in range(GLA_HEADS)],
        vmask=[(vl == h).astype(F32) for h in range(GLA_HEADS)],
        blockdiag=(kr == vc).astype(F32),
        ones=jnp.ones((c, GLA_HEADS * GLA_DV), BF16),
    )


def _gla_chunk(q, k, v, g, state, cst, fwd):
    c = GLA_CHUNK
    tri = cst["tril"] if fwd else cst["triu"]
    g_hi = g.astype(BF16)
    g_lo = (g - g_hi.astype(F32)).astype(BF16)
    bcum = _dot(tri, g_hi) + _dot(tri, g_lo)
    btot = bcum[c - 1:c, :] if fwd else bcum[0:1, :]
    qe = q * jnp.exp(bcum)
    ke = k * jnp.exp(-bcum)
    kd = k * jnp.exp(btot - bcum)
    vb = v.astype(BF16)
    qstack = jnp.concatenate([qe * cst["kmask"][h] for h in range(GLA_HEADS)], axis=0).astype(BF16)
    att = _dot_nt(qstack, ke.astype(BF16))
    att = jnp.where(cst["mask4_fw"] if fwd else cst["mask4_bw"], att, 0.0)
    full = _dot(att.astype(BF16), vb)
    o = full[0:c] * cst["vmask"][0]
    for h in range(1, GLA_HEADS):
        o = o + full[h * c:(h + 1) * c] * cst["vmask"][h]
    o = o + _dot(qe.astype(BF16), state.astype(BF16))
    dstate = _dot_tn(kd.astype(BF16), vb) * cst["blockdiag"]
    decay = jnp.exp(_dot_tn(g_hi, cst["ones"]) + _dot_tn(g_lo, cst["ones"]))
    return o, decay * state + dstate


def _meta_chunk(ref):
    x = ref[...]
    return jnp.concatenate([jnp.zeros((GLA_CHUNK - N_META, x.shape[1]), x.dtype), x], axis=0)


def _gla_fw_body(q_ref, k_ref, v_ref, g_ref, qm_ref, km_ref, vm_ref, gm_ref, o_ref, om_ref, state):
    cst = _gla_consts()

    @pl.when(pl.program_id(1) == 0)
    def _():
        o, s = _gla_chunk(_meta_chunk(qm_ref), _meta_chunk(km_ref), _meta_chunk(vm_ref), _meta_chunk(gm_ref),
                          jnp.zeros(state.shape, F32), cst, True)
        om_ref[...] = o[GLA_CHUNK - N_META:]
        state[...] = s

    s = state[...]
    for c in range(TM // GLA_CHUNK):
        sl = slice(c * GLA_CHUNK, (c + 1) * GLA_CHUNK)
        o, s = _gla_chunk(q_ref[sl, :], k_ref[sl, :], v_ref[sl, :], g_ref[sl, :], s, cst, True)
        o_ref[sl, :] = o
    state[...] = s


def _gla_finish(o, gate, w, bd):
    x2 = o * o
    hi = x2.astype(BF16)
    lo = (x2 - hi.astype(F32)).astype(BF16)
    ms = _dot(hi, bd) + _dot(lo, bd)
    return (o * lax.rsqrt(ms + NORM_EPS) * w * (gate * jax.nn.sigmoid(gate))).astype(BF16)


def _gla_bw_body(q_ref, k_ref, v_ref, g_ref, ofw_ref, gate_ref, qm_ref, km_ref, vm_ref, gm_ref, ofwm_ref, gatem_ref,
                 w_ref, o_ref, om_ref, state):
    cst = _gla_consts()
    t = pl.program_id(1)
    r = lax.broadcasted_iota(jnp.int32, (GLA_HEADS * GLA_DV,) * 2, 0) // GLA_DV
    c_ = lax.broadcasted_iota(jnp.int32, (GLA_HEADS * GLA_DV,) * 2, 1) // GLA_DV
    bd = jnp.where(r == c_, 1.0 / GLA_DV, 0.0).astype(BF16)

    @pl.when(t == 0)
    def _():
        state[...] = jnp.zeros(state.shape, F32)

    s = state[...]
    for c in reversed(range(TM // GLA_CHUNK)):
        sl = slice(c * GLA_CHUNK, (c + 1) * GLA_CHUNK)
        o, s = _gla_chunk(q_ref[sl, :], k_ref[sl, :], v_ref[sl, :], g_ref[sl, :], s, cst, False)
        o_ref[sl, :] = _gla_finish(o + ofw_ref[sl, :], gate_ref[sl, :], w_ref[...], bd)
    state[...] = s

    @pl.when(t == pl.num_programs(1) - 1)
    def _():
        o, _ = _gla_chunk(_meta_chunk(qm_ref), _meta_chunk(km_ref), _meta_chunk(vm_ref), _meta_chunk(gm_ref),
                          s, cst, False)
        om_ref[...] = _gla_finish(o[GLA_CHUNK - N_META:] + ofwm_ref[...], gatem_ref[...], w_ref[...], bd)


def _gla_call(lq, lk, lv, gfw, gbw, lgate, wnorm, g):
    B, S = g["B"], g["S"]
    nt = S // TM
    main0 = g["main_off"] // TM
    meta0 = g["meta_off"] // N_META
    mrow = lambda w, rev: pl.BlockSpec(
        (TM, w), (lambda b, t: (main0 + b * nt + (nt - 1 - t), 0)) if rev else (lambda b, t: (main0 + b * nt + t, 0)))
    meta = lambda w: pl.BlockSpec((N_META, w), lambda b, t: (meta0 + b, 0))
    orow = lambda w, rev: pl.BlockSpec(
        (TM, w), (lambda b, t: (b * nt + (nt - 1 - t), 0)) if rev else (lambda b, t: (b * nt + t, 0)))
    ometa = lambda w: pl.BlockSpec((N_META, w), lambda b, t: (b, 0))
    sem = _params(("arbitrary", "arbitrary"))
    state = [pltpu.VMEM((GLA_HEADS * GLA_DK, GLA_HEADS * GLA_DV), F32)]
    ofw, ofw_meta = pl.pallas_call(
        _gla_fw_body, grid=(B, nt),
        in_specs=[mrow(128, False), mrow(128, False), mrow(256, False), mrow(128, False),
                  meta(128), meta(128), meta(256), meta(128)],
        out_specs=(orow(256, False), ometa(256)),
        out_shape=(jax.ShapeDtypeStruct((B * S, 256), F32), jax.ShapeDtypeStruct((B * N_META, 256), F32)),
        scratch_shapes=state, compiler_params=sem, name="gla_fw_" + g["name"],
    )(lq, lk, lv, gfw, lq, lk, lv, gfw)
    o, o_meta = pl.pallas_call(
        _gla_bw_body, grid=(B, nt),
        in_specs=[mrow(128, True), mrow(128, True), mrow(256, True), mrow(128, True), orow(256, True), mrow(256, True),
                  meta(128), meta(128), meta(256), meta(128), ometa(256), meta(256), _const_spec(wnorm.shape)],
        out_specs=(orow(256, True), ometa(256)),
        out_shape=(jax.ShapeDtypeStruct((B * S, 256), BF16), jax.ShapeDtypeStruct((B * N_META, 256), BF16)),
        scratch_shapes=state, compiler_params=sem, name="gla_bw_" + g["name"],
    )(lq, lk, lv, gbw, ofw, lgate, lq, lk, lv, gbw, ofw_meta, lgate, wnorm)
    return o, o_meta


def _rot_cols(w, n):
    s = w.shape
    w4 = w.reshape(s[:-1] + (s[-1] // (2 * n), 2, n))
    return jnp.stack([-w4[..., 1, :], w4[..., 0, :]], axis=-2).reshape(s)


def _swap_halves(w, n):
    w3 = w.reshape(-1, 2, n)
    return w3[:, ::-1, :].reshape(w.shape)


def _layout(bp, sp, bs, ss):
    up = lambda x: -(-x // TM) * TM
    lay = {"p": dict(name="p", B=bp, S=sp, main_off=0, main_rows=bp * sp),
           "s": dict(name="s", B=bs, S=ss, main_off=bp * sp, main_rows=bs * ss)}
    lay["p"]["meta_off"] = bp * sp + bs * ss
    lay["s"]["meta_off"] = lay["p"]["meta_off"] + up(bp * N_META)
    lay["R"] = lay["s"]["meta_off"] + up(bs * N_META)
    for g in (lay["p"], lay["s"]):
        assert g["S"] % TM == 0 and g["main_off"] % g["S"] == 0 and g["S"] % GRID_W == 0
    return lay


def _tile_table(lay):
    smax = max(lay["p"]["S"], lay["s"]["S"])
    tab = []
    for g in (lay["p"], lay["s"]):
        tab += [t % (g["S"] // TM) for t in range(g["main_rows"] // TM)]
    tab += [smax // TM] * ((lay["R"] - lay["p"]["meta_off"]) // TM)
    return jnp.asarray(np.asarray(tab, np.int32)), smax


def _rope_tables(smax):
    r = jnp.arange(smax, dtype=jnp.int32)
    meta = jnp.arange(TM, dtype=jnp.int32) % N_META
    zero = jnp.zeros((TM,), jnp.int32)
    pos = jnp.concatenate([r + N_META, meta]).astype(F32)
    row = jnp.concatenate([r // GRID_W, zero]).astype(F32)
    col = jnp.concatenate([r % GRID_W, zero]).astype(F32)
    inv_m = ROPE_THETA ** (-jnp.arange(0, MLA_ROPE, 2, dtype=F32) / MLA_ROPE)
    half = GQA_HEAD_DIM // 2
    inv_g = ROPE_THETA ** (-jnp.arange(0, half, 2, dtype=F32) / half)
    a1, ar, ac = pos[:, None] * inv_m, row[:, None] * inv_g, col[:, None] * inv_g
    n = pos.shape[0]
    one, zer = jnp.ones((n, MLA_NOPE), F32), jnp.zeros((n, MLA_NOPE), F32)
    pad = jnp.zeros((n, HEAD_PAD - MLA_NOPE - MLA_ROPE), F32)
    cm = jnp.concatenate([one, jnp.cos(a1), jnp.cos(a1), pad], axis=1)
    sm = jnp.concatenate([zer, jnp.sin(a1), jnp.sin(a1), pad], axis=1)
    cg = jnp.concatenate([jnp.cos(ar), jnp.cos(ar), jnp.cos(ac), jnp.cos(ac)] * 2, axis=1)
    sg = jnp.concatenate([jnp.sin(ar), jnp.sin(ar), jnp.sin(ac), jnp.sin(ac)] * 2, axis=1)
    return dict(cm=cm, sm=sm, cg=cg, sg=sg)


def _mixer_consts(i, mix_norm, w_in, mla_q_norm, mla_w_uq, mla_kv_norm, mla_w_ukv, gqa_q_norm, gqa_k_norm,
                  gate_fw_w, gate_fw_b, gate_bw_w, gate_bw_b):
    w = w_in[i]
    z = lambda n: jnp.zeros((D_MODEL, n), F32)
    o = np.cumsum([0, MLA_Q_LORA, MLA_KV_LORA, MLA_ROPE, 384, 128, 128, 128, 128, 256, 16, 16, 256])
    seg = lambda j: w[:, o[j]:o[j + 1]]
    w_kr, w_gq, w_gk = seg(2), seg(3), seg(4)
    w_gqr = _rot_cols(w_gq, 16)
    tail = HEAD_PAD - MLA_NOPE - MLA_ROPE

    def gq_blocks(m):
        out = []
        for h in range(GQA_HEADS):
            blk = m[:, h * GQA_HEAD_DIM:(h + 1) * GQA_HEAD_DIM]
            out += [blk, z(GQA_HEAD_DIM)] if h < GQA_HEADS // GQA_KV_HEADS else [z(GQA_HEAD_DIM), blk]
        return out

    wbig = jnp.concatenate(
        [seg(0), seg(1), z(MLA_NOPE), w_kr, z(tail), z(MLA_NOPE), _rot_cols(w_kr, 16), z(tail)]
        + gq_blocks(w_gq) + gq_blocks(w_gqr)
        + [w_gk, _rot_cols(w_gk, 16), seg(6), seg(7), seg(8), seg(11), seg(9), seg(10), z(128 - 2 * GLA_GATE_RANK)],
        axis=1).astype(BF16)
    assert wbig.shape[1] == C_END

    uq = mla_w_uq[i].reshape(MLA_Q_LORA, MLA_HEADS, MLA_NOPE + MLA_ROPE)
    zq = jnp.zeros((MLA_Q_LORA, MLA_HEADS, tail), F32)
    wqa = jnp.concatenate([uq, zq], axis=2).reshape(MLA_Q_LORA, -1).astype(BF16)
    wqb = jnp.concatenate([jnp.zeros((MLA_Q_LORA, MLA_HEADS, MLA_NOPE), F32),
                           _rot_cols(uq[:, :, MLA_NOPE:], 16), zq], axis=2).reshape(MLA_Q_LORA, -1).astype(BF16)
    ukv = mla_w_ukv[i].reshape(MLA_KV_LORA, MLA_HEADS, MLA_NOPE + MLA_V)
    wka = jnp.concatenate([ukv[:, :, :MLA_NOPE], jnp.zeros((MLA_KV_LORA, MLA_HEADS, HEAD_PAD - MLA_NOPE), F32)],
                          axis=2).reshape(MLA_KV_LORA, -1).astype(BF16)
    wvt = ukv[:, :, MLA_NOPE:].reshape(MLA_KV_LORA, -1).T.astype(BF16)
    wgvt = seg(5).T.astype(BF16)

    gq, gk = gqa_q_norm[i], gqa_k_norm[i]
    gqs, gks = _swap_halves(gq, 16), _swap_halves(gk, 16)
    two = lambda v: jnp.concatenate([v, v])[None, :]
    wg = jnp.zeros((128, 256), F32)
    wg = wg.at[:GLA_GATE_RANK, :128].set(gate_fw_w[i]).at[GLA_GATE_RANK:2 * GLA_GATE_RANK, 128:].set(gate_bw_w[i])
    return dict(
        nw=mix_norm[i][None, :], wbig=wbig, qn=mla_q_norm[i][None, :], kvn=mla_kv_norm[i][None, :],
        wqa=wqa, wqb=wqb, wka=wka, wvt=wvt, gqc=two(gq), gqs=two(gqs), gkc=two(gk), gks=two(gks),
        wgvt=wgvt, wg=wg.astype(BF16), bg=jnp.concatenate([gate_fw_b[i], gate_bw_b[i]])[None, :])


def _assemble(lay, parts):
    pieces = [parts["p"][0], parts["s"][0]]
    for name, nxt in (("p", lay["s"]["meta_off"]), ("s", lay["R"])):
        m = parts[name][1][:lay[name]["B"] * N_META]
        pad = nxt - lay[name]["meta_off"] - m.shape[0]
        pieces += [m, jnp.zeros((pad, m.shape[1]), m.dtype)]
    return jnp.concatenate(pieces, axis=0)


def kernel(x_prompt, x_sample, meta_tokens, final_norm, ffn1_norm, ffn1_w_in, ffn1_w_out, mix_norm, w_in, w_out,
           mla_q_norm, mla_w_uq, mla_kv_norm, mla_w_ukv, gqa_q_norm, gqa_k_norm, gla_gate_fw_w, gla_gate_fw_b,
           gla_gate_bw_w, gla_gate_bw_b, gla_out_norm, ffn2_norm, ffn2_w_in, ffn2_w_out):
    bp, sp, _ = x_prompt.shape
    bs, ss, _ = x_sample.shape
    depth = w_in.shape[0]
    lay = _layout(bp, sp, bs, ss)
    tile_tab, smax = _tile_table(lay)
    tabs = _rope_tables(smax)
    h = _assemble(lay, {"p": (x_prompt.reshape(-1, D_MODEL), jnp.tile(meta_tokens, (bp, 1))),
                        "s": (x_sample.reshape(-1, D_MODEL), jnp.tile(meta_tokens, (bs, 1)))})
    wnorm = jnp.tile(gla_out_norm, (1, GLA_HEADS))
    n_mla = MLA_HEADS * MLA_V
    tq = {"p": min(TM, sp), "s": min(TM, ss)}

    def ffn_w(w_i, w_o):
        return w_i[:, :D_FF].astype(BF16), w_i[:, D_FF:].astype(BF16), w_o.astype(BF16)

    y = None
    for i in range(depth):
        h = _ffn_call(h, ffn1_norm[i][None, :], *ffn_w(ffn1_w_in[i], ffn1_w_out[i]))
        consts = _mixer_consts(i, mix_norm, w_in, mla_q_norm, mla_w_uq, mla_kv_norm, mla_w_ukv, gqa_q_norm,
                               gqa_k_norm, gla_gate_fw_w, gla_gate_fw_b, gla_gate_bw_w, gla_gate_bw_b)
        qm, km, vmt, qg, kg, vgt, lq, lk, lv, gfw, gbw, lgate = _mixin_call(h, tile_tab, consts, tabs)
        o_mla, o_gqa, o_gla = {}, {}, {}
        for name in ("p", "s"):
            g = lay[name]
            o_mla[name] = (_attn_call(qm, km, vmt, g, gqa=False, meta_q=False, tq=tq[name]),
                           _attn_call(qm, km, vmt, g, gqa=False, meta_q=True, tq=None))
            o_gqa[name] = (_attn_call(qg, kg, vgt, g, gqa=True, meta_q=False, tq=tq[name]),
                           _attn_call(qg, kg, vgt, g, gqa=True, meta_q=True, tq=None))
            o_gla[name] = _gla_call(lq, lk, lv, gfw, gbw, lgate, wnorm[i][None, :], g)
        wo = w_out[i].astype(BF16)
        proj = (_assemble(lay, o_mla), _assemble(lay, o_gqa), _assemble(lay, o_gla),
                wo[:n_mla], wo[n_mla:2 * n_mla], wo[2 * n_mla:])
        last = i == depth - 1
        out = _ffn_call(h, ffn2_norm[i][None, :], *ffn_w(ffn2_w_in[i], ffn2_w_out[i]), proj=proj,
                        final=final_norm[None, :] if last else None, lay=lay)
        if last:
            y = out
        else:
            h = out
    return y[0].reshape(bp, sp, D_MODEL), y[1].reshape(bs, ss, D_MODEL)
```

```python
import functools

import numpy as np
import jax
import jax.numpy as jnp
from jax import lax
from jax.experimental import pallas as pl
from jax.experimental.pallas import tpu as pltpu

F32 = jnp.float32
BF16 = jnp.bfloat16

D_MODEL = 1024
N_META = 16
GRID_W = 64
NORM_EPS = 1e-6
ROPE_THETA = 10000.0
D_FF = 2816
MLA_HEADS = 6
MLA_Q_LORA = 256
MLA_KV_LORA = 128
MLA_NOPE = 64
MLA_ROPE = 32
MLA_V = 64
GQA_HEADS = 6
GQA_KV_HEADS = 2
GQA_HEAD_DIM = 64
GLA_HEADS = 4
GLA_DK = 32
GLA_DV = 64
GLA_GATE_RANK = 16
GLA_GATE_NORM = 16.0
GLA_CHUNK = 64

LANES = 128
TM = 512
META_BLOCK = 128
HEAD_PAD = 128
VMEM_LIMIT = 56 * 1024 * 1024
FF_SPLITS = (0, 1536, D_FF)
ATTN_UNROLL = 8
ATTN_STATIC_CHUNKS = 4

LOG2E = 1.4426950408889634
MLA_SCALE = (MLA_NOPE + MLA_ROPE) ** -0.5 * LOG2E
GQA_SCALE = GQA_HEAD_DIM ** -0.5 * LOG2E
SUM_ROWS = 16
VROWS_MLA = SUM_ROWS + MLA_V
VROWS_GQA = GQA_KV_HEADS * (SUM_ROWS + GQA_HEAD_DIM)
GLA_QSCALE = GLA_DK ** -0.5

C_CQ = 0
C_CKV = C_CQ + MLA_Q_LORA
C_KP = C_CKV + MLA_KV_LORA
C_KPR = C_KP + HEAD_PAD
C_GQ = C_KPR + HEAD_PAD
C_GQR = C_GQ + GQA_HEADS * HEAD_PAD
C_GK = C_GQR + GQA_HEADS * HEAD_PAD
C_GKR = C_GK + 128
C_LQ = C_GKR + 128
C_LK = C_LQ + 128
C_LV = C_LK + 128
C_LGATE = C_LV + 256
C_LG = C_LGATE + 256
C_END = C_LG + 128

NT_DIMS = (((1,), (1,)), ((), ()))
TN_DIMS = (((0,), (0,)), ((), ()))


def _dot(a, b):
    return jnp.dot(a, b, preferred_element_type=F32)


def _dot_nt(a, b):
    return lax.dot_general(a, b, NT_DIMS, preferred_element_type=F32)


def _dot_tn(a, b):
    return lax.dot_general(a, b, TN_DIMS, preferred_element_type=F32)


def _rms(x, w):
    return x * lax.rsqrt(jnp.mean(x * x, axis=-1, keepdims=True) + NORM_EPS) * w


def _params(sem):
    return pltpu.CompilerParams(dimension_semantics=sem, vmem_limit_bytes=VMEM_LIMIT)


def _const_spec(shape):
    nd = len(shape)
    return pl.BlockSpec(shape, lambda *_: (0,) * nd, pipeline_mode=pl.Buffered(1))


def _ffn_body(*refs, has_proj, has_final, tiles_p, tiles_s):
    it = iter(refs)
    h_ref = next(it)
    if has_proj:
        oa_ref, ob_ref, oc_ref, wa_ref, wb_ref, wc_ref = (next(it) for _ in range(6))
    nw_ref, wg_ref, wu_ref, wo_ref = (next(it) for _ in range(4))
    if has_final:
        fw_ref, yp_ref, ys_ref = next(it), next(it), next(it)
    else:
        out_ref = next(it)

    h = h_ref[...]
    if has_proj:
        h = (h + _dot(oa_ref[...], wa_ref[...]) + _dot(ob_ref[...], wb_ref[...])
             + _dot(oc_ref[...], wc_ref[...]))
    xn = _rms(h, nw_ref[...]).astype(BF16)
    acc = jnp.zeros(h.shape, F32)
    for lo, hi in zip(FF_SPLITS[:-1], FF_SPLITS[1:]):
        g = _dot(xn, wg_ref[:, lo:hi])
        u = _dot(xn, wu_ref[:, lo:hi])
        a = (g * jax.nn.sigmoid(g) * u).astype(BF16)
        acc = acc + _dot(a, wo_ref[lo:hi, :])
    h2 = h + 0.5 * acc
    if has_final:
        y = _rms(h2, fw_ref[...])
        i = pl.program_id(0)

        @pl.when(i < tiles_p)
        def _():
            yp_ref[...] = y

        @pl.when(jnp.logical_and(i >= tiles_p, i < tiles_p + tiles_s))
        def _():
            ys_ref[...] = y
    else:
        out_ref[...] = h2


def _ffn_call(h, nw, wg, wu, wo, proj=None, final=None, lay=None):
    R = h.shape[0]
    n_tiles = R // TM
    row = lambda w: pl.BlockSpec((TM, w), lambda i: (i, 0))
    ins, specs = [h], [row(D_MODEL)]
    if proj is not None:
        oa, ob, oc, wa, wb, wc = proj
        ins += [oa, ob, oc, wa, wb, wc]
        specs += [row(oa.shape[1]), row(ob.shape[1]), row(oc.shape[1]),
                  _const_spec(wa.shape), _const_spec(wb.shape), _const_spec(wc.shape)]
    ins += [nw, wg, wu, wo]
    specs += [_const_spec(nw.shape), _const_spec(wg.shape), _const_spec(wu.shape), _const_spec(wo.shape)]
    tiles_p = tiles_s = 0
    if final is not None:
        tiles_p, tiles_s = lay["p"]["main_rows"] // TM, lay["s"]["main_rows"] // TM
        ins.append(final)
        specs.append(_const_spec(final.shape))
        out_shape = (jax.ShapeDtypeStruct((tiles_p * TM, D_MODEL), F32),
                     jax.ShapeDtypeStruct((tiles_s * TM, D_MODEL), F32))
        out_specs = (pl.BlockSpec((TM, D_MODEL), lambda i: (jnp.minimum(i, tiles_p - 1), 0)),
                     pl.BlockSpec((TM, D_MODEL), lambda i: (jnp.clip(i - tiles_p, 0, tiles_s - 1), 0)))
    else:
        out_shape = jax.ShapeDtypeStruct((R, D_MODEL), F32)
        out_specs = row(D_MODEL)
    body = functools.partial(_ffn_body, has_proj=proj is not None, has_final=final is not None,
                             tiles_p=tiles_p, tiles_s=tiles_s)
    return pl.pallas_call(
        body, grid=(n_tiles,), in_specs=specs, out_specs=out_specs, out_shape=out_shape,
        compiler_params=_params(("arbitrary",)),
        name="ffn" + ("_proj" if proj is not None else "") + ("_final" if final is not None else ""),
    )(*ins)


def _mixin_body(tab_ref, h_ref, nw_ref, wbig_ref, cm_ref, sm_ref, cg_ref, sg_ref,
                qn_ref, kvn_ref, wqa_ref, wqb_ref, wka_ref, wvt_ref,
                gqc_ref, gqs_ref, gkc_ref, gks_ref, wgvt_ref, wg_ref, bg_ref,
                qm_ref, km_ref, vmt_ref, qg_ref, kg_ref, vgt_ref,
                lq_ref, lk_ref, lv_ref, gfw_ref, gbw_ref, lgate_ref):
    del tab_ref
    hn = _rms(h_ref[...], nw_ref[...]).astype(BF16)

    pall = _dot(hn, wbig_ref[...])

    def proj(a, width):
        return pall[:, a:a + width]

    cm, sm = cm_ref[...], sm_ref[...]
    cg, sg = cg_ref[...], sg_ref[...]

    cqn = _rms(proj(C_CQ, MLA_Q_LORA), qn_ref[...]).astype(BF16)
    qa = _dot(cqn, wqa_ref[...])
    qb = _dot(cqn, wqb_ref[...])
    for hd in range(MLA_HEADS):
        blk = slice(hd * HEAD_PAD, (hd + 1) * HEAD_PAD)
        qm_ref[:, blk] = ((qa[:, blk] * cm + qb[:, blk] * sm) * MLA_SCALE).astype(BF16)

    kvn = _rms(proj(C_CKV, MLA_KV_LORA), kvn_ref[...]).astype(BF16)
    kpe = proj(C_KP, HEAD_PAD) * cm + proj(C_KPR, HEAD_PAD) * sm
    ka = _dot(kvn, wka_ref[...])
    for hd in range(MLA_HEADS):
        blk = slice(hd * HEAD_PAD, (hd + 1) * HEAD_PAD)
        km_ref[:, blk] = (ka[:, blk] + kpe).astype(BF16)
    ones = jnp.ones((SUM_ROWS, hn.shape[0]), F32)
    vt = _dot_nt(wvt_ref[...], kvn)
    pieces = []
    for hd in range(MLA_HEADS):
        pieces += [ones, vt[hd * MLA_V:(hd + 1) * MLA_V]]
    vmt_ref[0] = jnp.concatenate(pieces, axis=0).astype(BF16)

    tqc, tqs = gqc_ref[...] * cg, gqs_ref[...] * sg
    for hd in range(GQA_HEADS):
        x = proj(C_GQ + hd * HEAD_PAD, HEAD_PAD)
        xr = proj(C_GQR + hd * HEAD_PAD, HEAD_PAD)
        n = lax.rsqrt(jnp.sum(x * x, axis=-1, keepdims=True) * (1.0 / GQA_HEAD_DIM) + NORM_EPS)
        qg_ref[:, hd * HEAD_PAD:(hd + 1) * HEAD_PAD] = (n * (x * tqc + xr * tqs) * GQA_SCALE).astype(BF16)
    x = proj(C_GK, 128)
    xr = proj(C_GKR, 128)
    lo = lax.broadcasted_iota(jnp.int32, x.shape, 1) < GQA_HEAD_DIM
    x2 = x * x
    ms0 = jnp.sum(jnp.where(lo, x2, 0.0), axis=-1, keepdims=True) * (1.0 / GQA_HEAD_DIM)
    ms1 = jnp.sum(jnp.where(lo, 0.0, x2), axis=-1, keepdims=True) * (1.0 / GQA_HEAD_DIM)
    n = jnp.where(lo, lax.rsqrt(ms0 + NORM_EPS), lax.rsqrt(ms1 + NORM_EPS))
    kg_ref[...] = (n * (x * (gkc_ref[...] * cg) + xr * (gks_ref[...] * sg))).astype(BF16)
    vg = _dot_nt(wgvt_ref[...], hn)
    vgt_ref[0] = jnp.concatenate([ones, vg[:GQA_HEAD_DIM], ones, vg[GQA_HEAD_DIM:]], axis=0).astype(BF16)

    lq_ref[...] = proj(C_LQ, 128) * GLA_QSCALE
    lk_ref[...] = proj(C_LK, 128)
    lv_ref[...] = proj(C_LV, 256)
    lgate_ref[...] = proj(C_LGATE, 256)
    gg = _dot(proj(C_LG, 128).astype(BF16), wg_ref[...]) + bg_ref[...]
    ls = (jnp.minimum(gg, 0.0) - jnp.log1p(jnp.exp(-jnp.abs(gg)))) * (1.0 / GLA_GATE_NORM)
    gfw_ref[...] = ls[:, :128]
    gbw_ref[...] = ls[:, 128:]


def _mixin_call(h, tile_tab, consts, tabs):
    R = h.shape[0]
    n_tiles = R // TM
    row = lambda w: pl.BlockSpec((TM, w), lambda i, t: (i, 0))
    tabspec = pl.BlockSpec((TM, LANES), lambda i, t: (t[i], 0))
    cs = lambda a: _const_spec(a.shape)
    col3 = lambda r: pl.BlockSpec((1, r, TM), lambda i, t: (i, 0, 0))
    ins = [h, consts["nw"], consts["wbig"], tabs["cm"], tabs["sm"], tabs["cg"], tabs["sg"],
           consts["qn"], consts["kvn"], consts["wqa"], consts["wqb"], consts["wka"], consts["wvt"],
           consts["gqc"], consts["gqs"], consts["gkc"], consts["gks"], consts["wgvt"],
           consts["wg"], consts["bg"]]
    specs = [row(D_MODEL), cs(ins[1]), cs(ins[2]), tabspec, tabspec, tabspec, tabspec] + [cs(a) for a in ins[7:]]
    W6 = MLA_HEADS * HEAD_PAD
    out_shape = (
        jax.ShapeDtypeStruct((R, W6), BF16), jax.ShapeDtypeStruct((R, W6), BF16),
        jax.ShapeDtypeStruct((n_tiles, MLA_HEADS * VROWS_MLA, TM), BF16),
        jax.ShapeDtypeStruct((R, W6), BF16), jax.ShapeDtypeStruct((R, 128), BF16),
        jax.ShapeDtypeStruct((n_tiles, VROWS_GQA, TM), BF16),
        jax.ShapeDtypeStruct((R, 128), F32), jax.ShapeDtypeStruct((R, 128), F32),
        jax.ShapeDtypeStruct((R, 256), F32), jax.ShapeDtypeStruct((R, 128), F32),
        jax.ShapeDtypeStruct((R, 128), F32), jax.ShapeDtypeStruct((R, 256), F32),
    )
    out_specs = (row(W6), row(W6), col3(MLA_HEADS * VROWS_MLA), row(W6), row(128), col3(VROWS_GQA),
                 row(128), row(128), row(256), row(128), row(128), row(256))
    gs = pltpu.PrefetchScalarGridSpec(num_scalar_prefetch=1, grid=(n_tiles,), in_specs=specs, out_specs=out_specs)
    return pl.pallas_call(_mixin_body, grid_spec=gs, out_shape=out_shape,
                          compiler_params=_params(("arbitrary",)), name="mixer_in")(tile_tab, *ins)


def _attn_body(q_ref, k_ref, kmeta_ref, vt_ref, vtmeta_ref, o_ref, *scratch,
               gqa, tq, n_chunks, meta_q, b_axis, p_axis, unroll):
    b = pl.program_id(b_axis)
    p = pl.program_id(p_axis)
    slot = b % (META_BLOCK // N_META)
    rows = lax.broadcasted_iota(jnp.int32, (META_BLOCK, 1), 0)
    meta_valid = (rows // N_META) == slot
    halves = []
    for i in range(2):
        q = q_ref[:, i * HEAD_PAD:(i + 1) * HEAD_PAD]
        ks = slice(0, HEAD_PAD) if gqa else slice(i * HEAD_PAD, (i + 1) * HEAD_PAD)
        if gqa:
            group = (2 * p + i) // (GQA_HEADS // GQA_KV_HEADS)
            vs = pl.ds(pl.multiple_of(group * VROWS_MLA, SUM_ROWS), VROWS_MLA)
        else:
            vs = pl.ds(i * VROWS_MLA, VROWS_MLA)

        def update(carry, s_t, v_t):
            m, acc = carry
            m_new = jnp.maximum(m, jnp.max(s_t, axis=0, keepdims=True))
            p_t = jnp.exp2(s_t - m_new).astype(BF16)
            return m_new, jnp.exp2(m - m_new) * acc + _dot(v_t, p_t)

        def scores(c, ks=ks, q=q):
            return _dot_nt(k_ref[pl.ds(pl.multiple_of(c * TM, TM), TM), ks], q)

        def meta_scores(ks=ks, q=q):
            return jnp.where(meta_valid, _dot_nt(kmeta_ref[:, ks], q), -jnp.inf)

        carry = (jnp.full((1, tq), -jnp.inf, F32), jnp.zeros((VROWS_MLA, tq), F32))
        if n_chunks <= ATTN_STATIC_CHUNKS:
            s_cur = scores(0)
            for c in range(n_chunks):
                s_next = scores(c + 1) if c + 1 < n_chunks else meta_scores()
                carry = update(carry, s_cur, vt_ref[c, vs, :])
                s_cur = s_next
            s_meta = s_cur
        else:
            s_buf = scratch[-1]
            s_buf[0] = scores(0)

            def group(j, st, vs=vs):
                for u in range(unroll):
                    c = j * unroll + u
                    s_buf[(u + 1) % 2] = scores(jnp.minimum(c + 1, n_chunks - 1))
                    st = update(st, s_buf[u % 2], vt_ref[c, vs, :])
                return st

            carry = lax.fori_loop(0, n_chunks // unroll, group, carry)
            s_meta = meta_scores()
        _, acc = update(carry, s_meta, vtmeta_ref[0, vs, :])
        halves.append(acc[SUM_ROWS:] / acc[0:1])
    o = jnp.concatenate(halves, axis=0).T.astype(BF16)
    if meta_q:
        stage = scratch[0]

        @pl.when(slot == 0)
        def _():
            o_ref[...] = jnp.zeros(o_ref.shape, o_ref.dtype)

        stage[...] = o
        off = pl.multiple_of(slot * N_META, N_META)
        o_ref[pl.ds(off, N_META), :] = stage[pl.ds(off, N_META), :]
    else:
        o_ref[...] = o


def _attn_call(q, k, vt, g, *, gqa, meta_q, tq):
    B, S = g["B"], g["S"]
    n_chunks = S // TM
    kw = HEAD_PAD if gqa else 2 * HEAD_PAD
    vrows = VROWS_GQA if gqa else 2 * VROWS_MLA
    main_blk = g["main_off"] // S
    meta_blk = g["meta_off"] // META_BLOCK
    per_blk = META_BLOCK // N_META
    pk = (lambda p: 0) if gqa else (lambda p: p)
    if meta_q:
        grid = (3, B)
        ax = lambda f: (lambda p, b: f(b, p, 0))
        tq = META_BLOCK
        q_map = lambda b, p, t: (meta_blk + b // per_blk, p)
        o_map = lambda b, p, t: (b // per_blk, p)
        out_rows = pl.cdiv(B * N_META, META_BLOCK) * META_BLOCK
        sem = ("arbitrary", "arbitrary")
        scratch = [pltpu.VMEM((META_BLOCK, HEAD_PAD), BF16)]
        b_axis, p_axis = 1, 0
    else:
        nq = S // tq
        grid = (B, 3, nq)
        ax = lambda f: f
        q_map = lambda b, p, t: ((g["main_off"] + b * S) // tq + t, p)
        o_map = lambda b, p, t: (b * nq + t, p)
        out_rows = B * S
        sem = ("arbitrary", "arbitrary", "arbitrary")
        scratch = []
        b_axis, p_axis = 0, 1
    in_specs = [
        pl.BlockSpec((tq, 2 * HEAD_PAD), ax(q_map)),
        pl.BlockSpec((S, kw), ax(lambda b, p, t: (main_blk + b, pk(p)))),
        pl.BlockSpec((META_BLOCK, kw), ax(lambda b, p, t: (meta_blk + b // per_blk, pk(p)))),
        pl.BlockSpec((n_chunks, vrows, TM), ax(lambda b, p, t: (main_blk + b, pk(p), 0))),
        pl.BlockSpec((1, vrows, META_BLOCK),
                     ax(lambda b, p, t: (g["meta_off"] // TM + (b * N_META) // TM, pk(p),
                                         ((b * N_META) % TM) // META_BLOCK))),
    ]
    unroll = None
    if n_chunks > ATTN_STATIC_CHUNKS:
        unroll = max(u for u in range(2, ATTN_UNROLL + 1, 2) if n_chunks % u == 0)
        scratch = scratch + [pltpu.VMEM((2, TM, tq), F32)]
    body = functools.partial(_attn_body, gqa=gqa, tq=tq, n_chunks=n_chunks, meta_q=meta_q,
                             b_axis=b_axis, p_axis=p_axis, unroll=unroll)
    return pl.pallas_call(
        body, grid=grid, in_specs=in_specs,
        out_specs=pl.BlockSpec((tq, HEAD_PAD), ax(o_map)),
        out_shape=jax.ShapeDtypeStruct((out_rows, 3 * HEAD_PAD), BF16),
        scratch_shapes=scratch, compiler_params=_params(sem),
        name=("gqa" if gqa else "mla") + ("_metaq_" if meta_q else "_") + g["name"],
    )(q, k, k, vt, vt)


def _gla_consts():
    c = GLA_CHUNK
    r = lax.broadcasted_iota(jnp.int32, (c, c), 0)
    s = lax.broadcasted_iota(jnp.int32, (c, c), 1)
    r4 = lax.broadcasted_iota(jnp.int32, (GLA_HEADS * c, c), 0) % c
    s4 = lax.broadcasted_iota(jnp.int32, (GLA_HEADS * c, c), 1)
    kl = lax.broadcasted_iota(jnp.int32, (1, GLA_HEADS * GLA_DK), 1) // GLA_DK
    vl = lax.broadcasted_iota(jnp.int32, (1, GLA_HEADS * GLA_DV), 1) // GLA_DV
    kr = lax.broadcasted_iota(jnp.int32, (GLA_HEADS * GLA_DK, GLA_HEADS * GLA_DV), 0) // GLA_DK
    vc = lax.broadcasted_iota(jnp.int32, (GLA_HEADS * GLA_DK, GLA_HEADS * GLA_DV), 1) // GLA_DV
    return dict(
        tril=(s <= r).astype(BF16), triu=(s >= r).astype(BF16),
        mask4_fw=s4 <= r4, mask4_bw=s4 >= r4,
        kmask=[(kl == h).astype(F32) for h in range(GLA_HEADS)],
        vmask=[(vl == h).astype(F32) for h in range(GLA_HEADS)],
        blockdiag=(kr == vc).astype(F32),
        ones=jnp.ones((c, GLA_HEADS * GLA_DV), BF16),
    )


def _gla_chunks(qs, ks, vs, gs, state, cst, fwd):
    c, n = GLA_CHUNK, len(qs)
    nk = GLA_HEADS * GLA_DK
    tri = cst["tril"] if fwd else cst["triu"]
    hilo = []
    for g in gs:
        hi = g.astype(BF16)
        hilo.append(jnp.concatenate([hi, (g - hi.astype(F32)).astype(BF16)], axis=1))
    bcum = [_dot(tri, x) for x in hilo]
    bcum = [x[:, :nk] + x[:, nk:] for x in bcum]
    ltot = [_dot_tn(x, cst["ones"]) for x in hilo]
    decay = [jnp.exp(x[:nk] + x[nk:]) for x in ltot]
    qe = [q * jnp.exp(b) for q, b in zip(qs, bcum)]
    ke = [(k * jnp.exp(-b)).astype(BF16) for k, b in zip(ks, bcum)]
    kd = [(k * jnp.exp((b[c - 1:c] if fwd else b[0:1]) - b)).astype(BF16) for k, b in zip(ks, bcum)]
    vb = [v.astype(BF16) for v in vs]
    qstack = [jnp.concatenate([x * cst["kmask"][h] for h in range(GLA_HEADS)], axis=0).astype(BF16) for x in qe]
    mask = cst["mask4_fw"] if fwd else cst["mask4_bw"]
    att = [jnp.where(mask, _dot_nt(x, y), 0.0).astype(BF16) for x, y in zip(qstack, ke)]
    full = [_dot(x, y) for x, y in zip(att, vb)]
    dstate = [_dot_tn(x, y) * cst["blockdiag"] for x, y in zip(kd, vb)]
    states = []
    for i in range(n):
        states.append(state.astype(BF16))
        state = decay[i] * state + dstate[i]
    outs = []
    for i in range(n):
        o = _dot(qe[i].astype(BF16), states[i])
        for h in range(GLA_HEADS):
            o = o + full[i][h * c:(h + 1) * c] * cst["vmask"][h]
        outs.append(o)
    return outs, state


def _meta_chunk(ref):
    x = ref[...]
    return jnp.concatenate([jnp.zeros((GLA_CHUNK - N_META, x.shape[1]), x.dtype), x], axis=0)


def _gla_fw_body(q_ref, k_ref, v_ref, g_ref, qm_ref, km_ref, vm_ref, gm_ref, o_ref, om_ref, state):
    cst = _gla_consts()

    @pl.when(pl.program_id(1) == 0)
    def _():
        (o,), s = _gla_chunks([_meta_chunk(qm_ref)], [_meta_chunk(km_ref)], [_meta_chunk(vm_ref)],
                              [_meta_chunk(gm_ref)], jnp.zeros(state.shape, F32), cst, True)
        om_ref[...] = o[GLA_CHUNK - N_META:]
        state[...] = s

    sls = [slice(c * GLA_CHUNK, (c + 1) * GLA_CHUNK) for c in range(TM // GLA_CHUNK)]
    outs, s = _gla_chunks([q_ref[sl, :] for sl in sls], [k_ref[sl, :] for sl in sls], [v_ref[sl, :] for sl in sls],
                          [g_ref[sl, :] for sl in sls], state[...], cst, True)
    for sl, o in zip(sls, outs):
        o_ref[sl, :] = o
    state[...] = s


def _gla_finish(o, gate, w, bd):
    x2 = o * o
    hi = x2.astype(BF16)
    lo = (x2 - hi.astype(F32)).astype(BF16)
    ms = _dot(hi, bd) + _dot(lo, bd)
    return (o * lax.rsqrt(ms + NORM_EPS) * w * (gate * jax.nn.sigmoid(gate))).astype(BF16)


def _gla_bw_body(q_ref, k_ref, v_ref, g_ref, ofw_ref, gate_ref, qm_ref, km_ref, vm_ref, gm_ref, ofwm_ref, gatem_ref,
                 w_ref, o_ref, om_ref, state):
    cst = _gla_consts()
    t = pl.program_id(1)
    r = lax.broadcasted_iota(jnp.int32, (GLA_HEADS * GLA_DV,) * 2, 0) // GLA_DV
    c_ = lax.broadcasted_iota(jnp.int32, (GLA_HEADS * GLA_DV,) * 2, 1) // GLA_DV
    bd = jnp.where(r == c_, 1.0 / GLA_DV, 0.0).astype(BF16)

    @pl.when(t == 0)
    def _():
        state[...] = jnp.zeros(state.shape, F32)

    sls = [slice(c * GLA_CHUNK, (c + 1) * GLA_CHUNK) for c in reversed(range(TM // GLA_CHUNK))]
    outs, s = _gla_chunks([q_ref[sl, :] for sl in sls], [k_ref[sl, :] for sl in sls], [v_ref[sl, :] for sl in sls],
                          [g_ref[sl, :] for sl in sls], state[...], cst, False)
    for sl, o in zip(sls, outs):
        o_ref[sl, :] = _gla_finish(o + ofw_ref[sl, :], gate_ref[sl, :], w_ref[...], bd)
    state[...] = s

    @pl.when(t == pl.num_programs(1) - 1)
    def _():
        (o,), _ = _gla_chunks([_meta_chunk(qm_ref)], [_meta_chunk(km_ref)], [_meta_chunk(vm_ref)],
                              [_meta_chunk(gm_ref)], s, cst, False)
        om_ref[...] = _gla_finish(o[GLA_CHUNK - N_META:] + ofwm_ref[...], gatem_ref[...], w_ref[...], bd)


def _gla_call(lq, lk, lv, gfw, gbw, lgate, wnorm, g):
    B, S = g["B"], g["S"]
    nt = S // TM
    main0 = g["main_off"] // TM
    meta0 = g["meta_off"] // N_META
    mrow = lambda w, rev: pl.BlockSpec(
        (TM, w), (lambda b, t: (main0 + b * nt + (nt - 1 - t), 0)) if rev else (lambda b, t: (main0 + b * nt + t, 0)))
    meta = lambda w: pl.BlockSpec((N_META, w), lambda b, t: (meta0 + b, 0))
    orow = lambda w, rev: pl.BlockSpec(
        (TM, w), (lambda b, t: (b * nt + (nt - 1 - t), 0)) if rev else (lambda b, t: (b * nt + t, 0)))
    ometa = lambda w: pl.BlockSpec((N_META, w), lambda b, t: (b, 0))
    sem = _params(("arbitrary", "arbitrary"))
    state = [pltpu.VMEM((GLA_HEADS * GLA_DK, GLA_HEADS * GLA_DV), F32)]
    ofw, ofw_meta = pl.pallas_call(
        _gla_fw_body, grid=(B, nt),
        in_specs=[mrow(128, False), mrow(128, False), mrow(256, False), mrow(128, False),
                  meta(128), meta(128), meta(256), meta(128)],
        out_specs=(orow(256, False), ometa(256)),
        out_shape=(jax.ShapeDtypeStruct((B * S, 256), F32), jax.ShapeDtypeStruct((B * N_META, 256), F32)),
        scratch_shapes=state, compiler_params=sem, name="gla_fw_" + g["name"],
    )(lq, lk, lv, gfw, lq, lk, lv, gfw)
    o, o_meta = pl.pallas_call(
        _gla_bw_body, grid=(B, nt),
        in_specs=[mrow(128, True), mrow(128, True), mrow(256, True), mrow(128, True), orow(256, True), mrow(256, True),
                  meta(128), meta(128), meta(256), meta(128), ometa(256), meta(256), _const_spec(wnorm.shape)],
        out_specs=(orow(256, True), ometa(256)),
        out_shape=(jax.ShapeDtypeStruct((B * S, 256), BF16), jax.ShapeDtypeStruct((B * N_META, 256), BF16)),
        scratch_shapes=state, compiler_params=sem, name="gla_bw_" + g["name"],
    )(lq, lk, lv, gbw, ofw, lgate, lq, lk, lv, gbw, ofw_meta, lgate, wnorm)
    return o, o_meta


def _rot_cols(w, n):
    s = w.shape
    w4 = w.reshape(s[:-1] + (s[-1] // (2 * n), 2, n))
    return jnp.stack([-w4[..., 1, :], w4[..., 0, :]], axis=-2).reshape(s)


def _swap_halves(w, n):
    w3 = w.reshape(-1, 2, n)
    return w3[:, ::-1, :].reshape(w.shape)


def _layout(bp, sp, bs, ss):
    up = lambda x: -(-x // TM) * TM
    lay = {"p": dict(name="p", B=bp, S=sp, main_off=0, main_rows=bp * sp),
           "s": dict(name="s", B=bs, S=ss, main_off=bp * sp, main_rows=bs * ss)}
    lay["p"]["meta_off"] = bp * sp + bs * ss
    lay["s"]["meta_off"] = lay["p"]["meta_off"] + up(bp * N_META)
    lay["R"] = lay["s"]["meta_off"] + up(bs * N_META)
    for g in (lay["p"], lay["s"]):
        assert g["S"] % TM == 0 and g["main_off"] % g["S"] == 0 and g["S"] % GRID_W == 0
    return lay


def _tile_table(lay):
    smax = max(lay["p"]["S"], lay["s"]["S"])
    tab = []
    for g in (lay["p"], lay["s"]):
        tab += [t % (g["S"] // TM) for t in range(g["main_rows"] // TM)]
    tab += [smax // TM] * ((lay["R"] - lay["p"]["meta_off"]) // TM)
    return jnp.asarray(np.asarray(tab, np.int32)), smax


def _rope_tables(smax):
    r = jnp.arange(smax, dtype=jnp.int32)
    meta = jnp.arange(TM, dtype=jnp.int32) % N_META
    zero = jnp.zeros((TM,), jnp.int32)
    pos = jnp.concatenate([r + N_META, meta]).astype(F32)
    row = jnp.concatenate([r // GRID_W, zero]).astype(F32)
    col = jnp.concatenate([r % GRID_W, zero]).astype(F32)
    inv_m = ROPE_THETA ** (-jnp.arange(0, MLA_ROPE, 2, dtype=F32) / MLA_ROPE)
    half = GQA_HEAD_DIM // 2
    inv_g = ROPE_THETA ** (-jnp.arange(0, half, 2, dtype=F32) / half)
    a1, ar, ac = pos[:, None] * inv_m, row[:, None] * inv_g, col[:, None] * inv_g
    n = pos.shape[0]
    one, zer = jnp.ones((n, MLA_NOPE), F32), jnp.zeros((n, MLA_NOPE), F32)
    pad = jnp.zeros((n, HEAD_PAD - MLA_NOPE - MLA_ROPE), F32)
    cm = jnp.concatenate([one, jnp.cos(a1), jnp.cos(a1), pad], axis=1)
    sm = jnp.concatenate([zer, jnp.sin(a1), jnp.sin(a1), pad], axis=1)
    cg = jnp.concatenate([jnp.cos(ar), jnp.cos(ar), jnp.cos(ac), jnp.cos(ac)] * 2, axis=1)
    sg = jnp.concatenate([jnp.sin(ar), jnp.sin(ar), jnp.sin(ac), jnp.sin(ac)] * 2, axis=1)
    return dict(cm=cm, sm=sm, cg=cg, sg=sg)


def _mixer_consts(i, mix_norm, w_in, mla_q_norm, mla_w_uq, mla_kv_norm, mla_w_ukv, gqa_q_norm, gqa_k_norm,
                  gate_fw_w, gate_fw_b, gate_bw_w, gate_bw_b):
    w = w_in[i]
    z = lambda n: jnp.zeros((D_MODEL, n), F32)
    o = np.cumsum([0, MLA_Q_LORA, MLA_KV_LORA, MLA_ROPE, 384, 128, 128, 128, 128, 256, 16, 16, 256])
    seg = lambda j: w[:, o[j]:o[j + 1]]
    w_kr, w_gq, w_gk = seg(2), seg(3), seg(4)
    w_gqr = _rot_cols(w_gq, 16)
    tail = HEAD_PAD - MLA_NOPE - MLA_ROPE

    def gq_blocks(m):
        out = []
        for h in range(GQA_HEADS):
            blk = m[:, h * GQA_HEAD_DIM:(h + 1) * GQA_HEAD_DIM]
            out += [blk, z(GQA_HEAD_DIM)] if h < GQA_HEADS // GQA_KV_HEADS else [z(GQA_HEAD_DIM), blk]
        return out

    wbig = jnp.concatenate(
        [seg(0), seg(1), z(MLA_NOPE), w_kr, z(tail), z(MLA_NOPE), _rot_cols(w_kr, 16), z(tail)]
        + gq_blocks(w_gq) + gq_blocks(w_gqr)
        + [w_gk, _rot_cols(w_gk, 16), seg(6), seg(7), seg(8), seg(11), seg(9), seg(10), z(128 - 2 * GLA_GATE_RANK)],
        axis=1).astype(BF16)
    assert wbig.shape[1] == C_END

    uq = mla_w_uq[i].reshape(MLA_Q_LORA, MLA_HEADS, MLA_NOPE + MLA_ROPE)
    zq = jnp.zeros((MLA_Q_LORA, MLA_HEADS, tail), F32)
    wqa = jnp.concatenate([uq, zq], axis=2).reshape(MLA_Q_LORA, -1).astype(BF16)
    wqb = jnp.concatenate([jnp.zeros((MLA_Q_LORA, MLA_HEADS, MLA_NOPE), F32),
                           _rot_cols(uq[:, :, MLA_NOPE:], 16), zq], axis=2).reshape(MLA_Q_LORA, -1).astype(BF16)
    ukv = mla_w_ukv[i].reshape(MLA_KV_LORA, MLA_HEADS, MLA_NOPE + MLA_V)
    wka = jnp.concatenate([ukv[:, :, :MLA_NOPE], jnp.zeros((MLA_KV_LORA, MLA_HEADS, HEAD_PAD - MLA_NOPE), F32)],
                          axis=2).reshape(MLA_KV_LORA, -1).astype(BF16)
    wvt = ukv[:, :, MLA_NOPE:].reshape(MLA_KV_LORA, -1).T.astype(BF16)
    wgvt = seg(5).T.astype(BF16)

    gq, gk = gqa_q_norm[i], gqa_k_norm[i]
    gqs, gks = _swap_halves(gq, 16), _swap_halves(gk, 16)
    two = lambda v: jnp.concatenate([v, v])[None, :]
    wg = jnp.zeros((128, 256), F32)
    wg = wg.at[:GLA_GATE_RANK, :128].set(gate_fw_w[i]).at[GLA_GATE_RANK:2 * GLA_GATE_RANK, 128:].set(gate_bw_w[i])
    return dict(
        nw=mix_norm[i][None, :], wbig=wbig, qn=mla_q_norm[i][None, :], kvn=mla_kv_norm[i][None, :],
        wqa=wqa, wqb=wqb, wka=wka, wvt=wvt, gqc=two(gq), gqs=two(gqs), gkc=two(gk), gks=two(gks),
        wgvt=wgvt, wg=wg.astype(BF16), bg=jnp.concatenate([gate_fw_b[i], gate_bw_b[i]])[None, :])


def _assemble(lay, parts):
    pieces = [parts["p"][0], parts["s"][0]]
    for name, nxt in (("p", lay["s"]["meta_off"]), ("s", lay["R"])):
        m = parts[name][1][:lay[name]["B"] * N_META]
        pad = nxt - lay[name]["meta_off"] - m.shape[0]
        pieces += [m, jnp.zeros((pad, m.shape[1]), m.dtype)]
    return jnp.concatenate(pieces, axis=0)


def kernel(x_prompt, x_sample, meta_tokens, final_norm, ffn1_norm, ffn1_w_in, ffn1_w_out, mix_norm, w_in, w_out,
           mla_q_norm, mla_w_uq, mla_kv_norm, mla_w_ukv, gqa_q_norm, gqa_k_norm, gla_gate_fw_w, gla_gate_fw_b,
           gla_gate_bw_w, gla_gate_bw_b, gla_out_norm, ffn2_norm, ffn2_w_in, ffn2_w_out):
    bp, sp, _ = x_prompt.shape
    bs, ss, _ = x_sample.shape
    depth = w_in.shape[0]
    lay = _layout(bp, sp, bs, ss)
    tile_tab, smax = _tile_table(lay)
    tabs = _rope_tables(smax)
    h = _assemble(lay, {"p": (x_prompt.reshape(-1, D_MODEL), jnp.tile(meta_tokens, (bp, 1))),
                        "s": (x_sample.reshape(-1, D_MODEL), jnp.tile(meta_tokens, (bs, 1)))})
    wnorm = jnp.tile(gla_out_norm, (1, GLA_HEADS))
    n_mla = MLA_HEADS * MLA_V
    tq = {"p": min(TM, sp), "s": min(TM, ss)}

    def ffn_w(w_i, w_o):
        return w_i[:, :D_FF].astype(BF16), w_i[:, D_FF:].astype(BF16), w_o.astype(BF16)

    y = None
    for i in range(depth):
        h = _ffn_call(h, ffn1_norm[i][None, :], *ffn_w(ffn1_w_in[i], ffn1_w_out[i]))
        consts = _mixer_consts(i, mix_norm, w_in, mla_q_norm, mla_w_uq, mla_kv_norm, mla_w_ukv, gqa_q_norm,
                               gqa_k_norm, gla_gate_fw_w, gla_gate_fw_b, gla_gate_bw_w, gla_gate_bw_b)
        qm, km, vmt, qg, kg, vgt, lq, lk, lv, gfw, gbw, lgate = _mixin_call(h, tile_tab, consts, tabs)
        o_mla, o_gqa, o_gla = {}, {}, {}
        for name in ("p", "s"):
            g = lay[name]
            o_mla[name] = (_attn_call(qm, km, vmt, g, gqa=False, meta_q=False, tq=tq[name]),
                           _attn_call(qm, km, vmt, g, gqa=False, meta_q=True, tq=None))
            o_gqa[name] = (_attn_call(qg, kg, vgt, g, gqa=True, meta_q=False, tq=tq[name]),
                           _attn_call(qg, kg, vgt, g, gqa=True, meta_q=True, tq=None))
            o_gla[name] = _gla_call(lq, lk, lv, gfw, gbw, lgate, wnorm[i][None, :], g)
        wo = w_out[i].astype(BF16)
        proj = (_assemble(lay, o_mla), _assemble(lay, o_gqa), _assemble(lay, o_gla),
                wo[:n_mla], wo[n_mla:2 * n_mla], wo[2 * n_mla:])
        last = i == depth - 1
        out = _ffn_call(h, ffn2_norm[i][None, :], *ffn_w(ffn2_w_in[i], ffn2_w_out[i]), proj=proj,
                        final=final_norm[None, :] if last else None, lay=lay)
        if last:
            y = out
        else:
            h = out
    return y[0].reshape(bp, sp, D_MODEL), y[1].reshape(bs, ss, D_MODEL)
```

```python
import functools

import numpy as np
import jax
import jax.numpy as jnp
from jax import lax
from jax.experimental import pallas as pl
from jax.experimental.pallas import tpu as pltpu

F32 = jnp.float32
BF16 = jnp.bfloat16

D_MODEL = 1024
N_META = 16
GRID_W = 64
NORM_EPS = 1e-6
ROPE_THETA = 10000.0
D_FF = 2816
MLA_HEADS = 6
MLA_Q_LORA = 256
MLA_KV_LORA = 128
MLA_NOPE = 64
MLA_ROPE = 32
MLA_V = 64
GQA_HEADS = 6
GQA_KV_HEADS = 2
GQA_HEAD_DIM = 64
GLA_HEADS = 4
GLA_DK = 32
GLA_DV = 64
GLA_GATE_RANK = 16
GLA_GATE_NORM = 16.0
GLA_CHUNK = 64

LANES = 128
TM = 512
META_BLOCK = 128
HEAD_PAD = 128
VMEM_LIMIT = 56 * 1024 * 1024
FF_SPLITS = (0, 1536, D_FF)
ATTN_TK = 512
ATTN_MAX_EXCESS = 64.0
ATTN_UNROLL = 16
ATTN_STATIC_CHUNKS = 4

LOG2E = 1.4426950408889634
MLA_SCALE = (MLA_NOPE + MLA_ROPE) ** -0.5 * LOG2E
GQA_SCALE = GQA_HEAD_DIM ** -0.5 * LOG2E
SUM_ROWS = 16
VROWS_MLA = SUM_ROWS + MLA_V
VROWS_GQA = GQA_KV_HEADS * (SUM_ROWS + GQA_HEAD_DIM)
GLA_QSCALE = GLA_DK ** -0.5

C_CQ = 0
C_CKV = C_CQ + MLA_Q_LORA
C_KP = C_CKV + MLA_KV_LORA
C_KPR = C_KP + HEAD_PAD
C_GQ = C_KPR + HEAD_PAD
C_GQR = C_GQ + GQA_HEADS * HEAD_PAD
C_GK = C_GQR + GQA_HEADS * HEAD_PAD
C_GKR = C_GK + 128
C_LQ = C_GKR + 128
C_LK = C_LQ + 128
C_LV = C_LK + 128
C_LGATE = C_LV + 256
C_LG = C_LGATE + 256
C_END = C_LG + 128

NT_DIMS = (((1,), (1,)), ((), ()))
TN_DIMS = (((0,), (0,)), ((), ()))


def _dot(a, b):
    return jnp.dot(a, b, preferred_element_type=F32)


def _dot_nt(a, b):
    return lax.dot_general(a, b, NT_DIMS, preferred_element_type=F32)


def _dot_tn(a, b):
    return lax.dot_general(a, b, TN_DIMS, preferred_element_type=F32)


def _rms(x, w):
    return x * lax.rsqrt(jnp.mean(x * x, axis=-1, keepdims=True) + NORM_EPS) * w


def _params(sem):
    return pltpu.CompilerParams(dimension_semantics=sem, vmem_limit_bytes=VMEM_LIMIT)


def _const_spec(shape):
    nd = len(shape)
    return pl.BlockSpec(shape, lambda *_: (0,) * nd, pipeline_mode=pl.Buffered(1))


def _ffn_body(*refs, has_proj, has_final, tiles_p, tiles_s):
    it = iter(refs)
    h_ref = next(it)
    if has_proj:
        oa_ref, ob_ref, oc_ref, wa_ref, wb_ref, wc_ref = (next(it) for _ in range(6))
    nw_ref, wg_ref, wu_ref, wo_ref = (next(it) for _ in range(4))
    if has_final:
        fw_ref, yp_ref, ys_ref = next(it), next(it), next(it)
    else:
        out_ref = next(it)

    h = h_ref[...]
    if has_proj:
        h = (h + _dot(oa_ref[...], wa_ref[...]) + _dot(ob_ref[...], wb_ref[...])
             + _dot(oc_ref[...], wc_ref[...]))
    xn = _rms(h, nw_ref[...]).astype(BF16)
    acc = jnp.zeros(h.shape, F32)
    for lo, hi in zip(FF_SPLITS[:-1], FF_SPLITS[1:]):
        g = _dot(xn, wg_ref[:, lo:hi])
        u = _dot(xn, wu_ref[:, lo:hi])
        a = (g * jax.nn.sigmoid(g) * u).astype(BF16)
        acc = acc + _dot(a, wo_ref[lo:hi, :])
    h2 = h + 0.5 * acc
    if has_final:
        y = _rms(h2, fw_ref[...])
        i = pl.program_id(0)

        @pl.when(i < tiles_p)
        def _():
            yp_ref[...] = y

        @pl.when(jnp.logical_and(i >= tiles_p, i < tiles_p + tiles_s))
        def _():
            ys_ref[...] = y
    else:
        out_ref[...] = h2


def _ffn_call(h, nw, wg, wu, wo, proj=None, final=None, lay=None):
    R = h.shape[0]
    n_tiles = R // TM
    row = lambda w: pl.BlockSpec((TM, w), lambda i: (i, 0))
    ins, specs = [h], [row(D_MODEL)]
    if proj is not None:
        oa, ob, oc, wa, wb, wc = proj
        ins += [oa, ob, oc, wa, wb, wc]
        specs += [row(oa.shape[1]), row(ob.shape[1]), row(oc.shape[1]),
                  _const_spec(wa.shape), _const_spec(wb.shape), _const_spec(wc.shape)]
    ins += [nw, wg, wu, wo]
    specs += [_const_spec(nw.shape), _const_spec(wg.shape), _const_spec(wu.shape), _const_spec(wo.shape)]
    tiles_p = tiles_s = 0
    if final is not None:
        tiles_p, tiles_s = lay["p"]["main_rows"] // TM, lay["s"]["main_rows"] // TM
        ins.append(final)
        specs.append(_const_spec(final.shape))
        out_shape = (jax.ShapeDtypeStruct((tiles_p * TM, D_MODEL), F32),
                     jax.ShapeDtypeStruct((tiles_s * TM, D_MODEL), F32))
        out_specs = (pl.BlockSpec((TM, D_MODEL), lambda i: (jnp.minimum(i, tiles_p - 1), 0)),
                     pl.BlockSpec((TM, D_MODEL), lambda i: (jnp.clip(i - tiles_p, 0, tiles_s - 1), 0)))
    else:
        out_shape = jax.ShapeDtypeStruct((R, D_MODEL), F32)
        out_specs = row(D_MODEL)
    body = functools.partial(_ffn_body, has_proj=proj is not None, has_final=final is not None,
                             tiles_p=tiles_p, tiles_s=tiles_s)
    return pl.pallas_call(
        body, grid=(n_tiles,), in_specs=specs, out_specs=out_specs, out_shape=out_shape,
        compiler_params=_params(("arbitrary",)),
        name="ffn" + ("_proj" if proj is not None else "") + ("_final" if final is not None else ""),
    )(*ins)


def _mixin_body(tab_ref, h_ref, nw_ref, wbig_ref, cm_ref, sm_ref, cg_ref, sg_ref,
                qn_ref, kvn_ref, wqa_ref, wqb_ref, wka_ref, wvt_ref,
                gqc_ref, gqs_ref, gkc_ref, gks_ref, wgvt_ref, wg_ref, bg_ref,
                qm_ref, km_ref, vmt_ref, qg_ref, kg_ref, vgt_ref,
                lq_ref, lk_ref, lv_ref, gfw_ref, gbw_ref, lgate_ref):
    del tab_ref
    hn = _rms(h_ref[...], nw_ref[...]).astype(BF16)

    pall = _dot(hn, wbig_ref[...])

    def proj(a, width):
        return pall[:, a:a + width]

    cm, sm = cm_ref[...], sm_ref[...]
    cg, sg = cg_ref[...], sg_ref[...]

    cqn = _rms(proj(C_CQ, MLA_Q_LORA), qn_ref[...]).astype(BF16)
    qa = _dot(cqn, wqa_ref[...])
    qb = _dot(cqn, wqb_ref[...])
    for hd in range(MLA_HEADS):
        blk = slice(hd * HEAD_PAD, (hd + 1) * HEAD_PAD)
        qm_ref[:, blk] = ((qa[:, blk] * cm + qb[:, blk] * sm) * MLA_SCALE).astype(BF16)

    kvn = _rms(proj(C_CKV, MLA_KV_LORA), kvn_ref[...]).astype(BF16)
    kpe = proj(C_KP, HEAD_PAD) * cm + proj(C_KPR, HEAD_PAD) * sm
    ka = _dot(kvn, wka_ref[...])
    for hd in range(MLA_HEADS):
        blk = slice(hd * HEAD_PAD, (hd + 1) * HEAD_PAD)
        km_ref[:, blk] = (ka[:, blk] + kpe).astype(BF16)
    ones = jnp.ones((SUM_ROWS, hn.shape[0]), F32)
    vt = _dot_nt(wvt_ref[...], kvn)
    pieces = []
    for hd in range(MLA_HEADS):
        pieces += [ones, vt[hd * MLA_V:(hd + 1) * MLA_V]]
    vmt_ref[0] = jnp.concatenate(pieces, axis=0).astype(BF16)

    tqc, tqs = gqc_ref[...] * cg, gqs_ref[...] * sg
    for hd in range(GQA_HEADS):
        x = proj(C_GQ + hd * HEAD_PAD, HEAD_PAD)
        xr = proj(C_GQR + hd * HEAD_PAD, HEAD_PAD)
        n = lax.rsqrt(jnp.sum(x * x, axis=-1, keepdims=True) * (1.0 / GQA_HEAD_DIM) + NORM_EPS)
        qg_ref[:, hd * HEAD_PAD:(hd + 1) * HEAD_PAD] = (n * (x * tqc + xr * tqs) * GQA_SCALE).astype(BF16)
    x = proj(C_GK, 128)
    xr = proj(C_GKR, 128)
    lo = lax.broadcasted_iota(jnp.int32, x.shape, 1) < GQA_HEAD_DIM
    x2 = x * x
    ms0 = jnp.sum(jnp.where(lo, x2, 0.0), axis=-1, keepdims=True) * (1.0 / GQA_HEAD_DIM)
    ms1 = jnp.sum(jnp.where(lo, 0.0, x2), axis=-1, keepdims=True) * (1.0 / GQA_HEAD_DIM)
    n = jnp.where(lo, lax.rsqrt(ms0 + NORM_EPS), lax.rsqrt(ms1 + NORM_EPS))
    kg_ref[...] = (n * (x * (gkc_ref[...] * cg) + xr * (gks_ref[...] * sg))).astype(BF16)
    vg = _dot_nt(wgvt_ref[...], hn)
    vgt_ref[0] = jnp.concatenate([ones, vg[:GQA_HEAD_DIM], ones, vg[GQA_HEAD_DIM:]], axis=0).astype(BF16)

    lq_ref[...] = proj(C_LQ, 128) * GLA_QSCALE
    lk_ref[...] = proj(C_LK, 128)
    lv_ref[...] = proj(C_LV, 256)
    lgate_ref[...] = proj(C_LGATE, 256)
    gg = _dot(proj(C_LG, 128).astype(BF16), wg_ref[...]) + bg_ref[...]
    ls = (jnp.minimum(gg, 0.0) - jnp.log1p(jnp.exp(-jnp.abs(gg)))) * (1.0 / GLA_GATE_NORM)
    gfw_ref[...] = ls[:, :128]
    gbw_ref[...] = ls[:, 128:]


def _mixin_call(h, tile_tab, consts, tabs):
    R = h.shape[0]
    n_tiles = R // TM
    row = lambda w: pl.BlockSpec((TM, w), lambda i, t: (i, 0))
    tabspec = pl.BlockSpec((TM, LANES), lambda i, t: (t[i], 0))
    cs = lambda a: _const_spec(a.shape)
    col3 = lambda r: pl.BlockSpec((1, r, TM), lambda i, t: (i, 0, 0))
    ins = [h, consts["nw"], consts["wbig"], tabs["cm"], tabs["sm"], tabs["cg"], tabs["sg"],
           consts["qn"], consts["kvn"], consts["wqa"], consts["wqb"], consts["wka"], consts["wvt"],
           consts["gqc"], consts["gqs"], consts["gkc"], consts["gks"], consts["wgvt"],
           consts["wg"], consts["bg"]]
    specs = [row(D_MODEL), cs(ins[1]), cs(ins[2]), tabspec, tabspec, tabspec, tabspec] + [cs(a) for a in ins[7:]]
    W6 = MLA_HEADS * HEAD_PAD
    out_shape = (
        jax.ShapeDtypeStruct((R, W6), BF16), jax.ShapeDtypeStruct((R, W6), BF16),
        jax.ShapeDtypeStruct((n_tiles, MLA_HEADS * VROWS_MLA, TM), BF16),
        jax.ShapeDtypeStruct((R, W6), BF16), jax.ShapeDtypeStruct((R, 128), BF16),
        jax.ShapeDtypeStruct((n_tiles, VROWS_GQA, TM), BF16),
        jax.ShapeDtypeStruct((R, 128), F32), jax.ShapeDtypeStruct((R, 128), F32),
        jax.ShapeDtypeStruct((R, 256), F32), jax.ShapeDtypeStruct((R, 128), F32),
        jax.ShapeDtypeStruct((R, 128), F32), jax.ShapeDtypeStruct((R, 256), F32),
    )
    out_specs = (row(W6), row(W6), col3(MLA_HEADS * VROWS_MLA), row(W6), row(128), col3(VROWS_GQA),
                 row(128), row(128), row(256), row(128), row(128), row(256))
    gs = pltpu.PrefetchScalarGridSpec(num_scalar_prefetch=1, grid=(n_tiles,), in_specs=specs, out_specs=out_specs)
    return pl.pallas_call(_mixin_body, grid_spec=gs, out_shape=out_shape,
                          compiler_params=_params(("arbitrary",)), name="mixer_in")(tile_tab, *ins)


def _attn_body(q_ref, k_ref, kmeta_ref, vt_ref, vtmeta_ref, o_ref, *scratch,
               gqa, tq, n_chunks, meta_q, b_axis, p_axis, unroll):
    b = pl.program_id(b_axis)
    p = pl.program_id(p_axis)
    slot = b % (META_BLOCK // N_META)
    rows = lax.broadcasted_iota(jnp.int32, (META_BLOCK, 1), 0)
    meta_valid = (rows // N_META) == slot
    zeros = jnp.zeros((VROWS_MLA, tq), F32)

    def exact_update(carry, s_t, v_t):
        m, acc = carry
        m_new = jnp.maximum(m, jnp.max(s_t, axis=0, keepdims=True))
        p_t = jnp.exp2(s_t - m_new).astype(BF16)
        return m_new, jnp.exp2(m - m_new) * acc + _dot(v_t, p_t)

    def fast_update(carry, s_t, v_t):
        m, acc, excess = carry
        p_t = jnp.exp2(s_t - m).astype(BF16)
        cmax = jnp.max(s_t, axis=0, keepdims=True)
        m_new = jnp.maximum(m, cmax)
        return m_new, jnp.exp2(m - m_new) * (acc + _dot(v_t, p_t)), jnp.maximum(excess, cmax - m)

    def head(i, fast):
        q = q_ref[:, i * HEAD_PAD:(i + 1) * HEAD_PAD]
        ks = slice(0, HEAD_PAD) if gqa else slice(i * HEAD_PAD, (i + 1) * HEAD_PAD)
        if gqa:
            group = (2 * p + i) // (GQA_HEADS // GQA_KV_HEADS)
            vs = pl.ds(pl.multiple_of(group * VROWS_MLA, SUM_ROWS), VROWS_MLA)
        else:
            vs = pl.ds(i * VROWS_MLA, VROWS_MLA)
        update = fast_update if fast else exact_update
        parts = TM // ATTN_TK

        def scores(c):
            return _dot_nt(k_ref[pl.ds(pl.multiple_of(c * ATTN_TK, ATTN_TK), ATTN_TK), ks], q)

        def values(tile, part):
            return vt_ref[tile, vs, part * ATTN_TK:(part + 1) * ATTN_TK]

        def meta_scores():
            return jnp.where(meta_valid, _dot_nt(kmeta_ref[:, ks], q), -jnp.inf)

        s_cur = scores(0)
        if fast:
            carry = (jnp.max(s_cur, axis=0, keepdims=True), zeros, jnp.zeros((1, tq), F32))
        else:
            carry = (jnp.full((1, tq), -jnp.inf, F32), zeros)
        if n_chunks <= ATTN_STATIC_CHUNKS:
            for c in range(n_chunks):
                s_next = scores(c + 1) if c + 1 < n_chunks else meta_scores()
                carry = update(carry, s_cur, values(c // parts, c % parts))
                s_cur = s_next
        else:
            s_buf = scratch[-1]
            s_buf[0] = s_cur

            def group_step(j, st):
                for u in range(unroll):
                    c = j * unroll + u
                    s_buf[(u + 1) % 2] = scores(jnp.minimum(c + 1, n_chunks - 1))
                    st = update(st, s_buf[u % 2], values(j * (unroll // parts) + u // parts, u % parts))
                return st

            carry = lax.fori_loop(0, n_chunks // unroll, group_step, carry)
            s_cur = meta_scores()
        return update(carry, s_cur, vtmeta_ref[0, vs, :])

    if n_chunks <= ATTN_STATIC_CHUNKS:
        fast = [head(i, True) for i in range(2)]
        worst = jnp.max(jnp.maximum(fast[0][2], fast[1][2]))
        accs = lax.cond(worst > ATTN_MAX_EXCESS,
                        lambda: tuple(head(i, False)[1] for i in range(2)),
                        lambda: tuple(f[1] for f in fast))
    else:
        accs = [head(i, False)[1] for i in range(2)]
    halves = [acc[SUM_ROWS:] / acc[0:1] for acc in accs]
    o = jnp.concatenate(halves, axis=0).T.astype(BF16)
    if meta_q:
        stage = scratch[0]

        @pl.when(slot == 0)
        def _():
            o_ref[...] = jnp.zeros(o_ref.shape, o_ref.dtype)

        stage[...] = o
        off = pl.multiple_of(slot * N_META, N_META)
        o_ref[pl.ds(off, N_META), :] = stage[pl.ds(off, N_META), :]
    else:
        o_ref[...] = o


def _attn_call(q, k, vt, g, *, gqa, meta_q, tq):
    B, S = g["B"], g["S"]
    n_chunks = S // ATTN_TK
    n_tiles = S // TM
    kw = HEAD_PAD if gqa else 2 * HEAD_PAD
    vrows = VROWS_GQA if gqa else 2 * VROWS_MLA
    main_blk = g["main_off"] // S
    meta_blk = g["meta_off"] // META_BLOCK
    per_blk = META_BLOCK // N_META
    pk = (lambda p: 0) if gqa else (lambda p: p)
    if meta_q:
        grid = (3, B)
        ax = lambda f: (lambda p, b: f(b, p, 0))
        tq = META_BLOCK
        q_map = lambda b, p, t: (meta_blk + b // per_blk, p)
        o_map = lambda b, p, t: (b // per_blk, p)
        out_rows = pl.cdiv(B * N_META, META_BLOCK) * META_BLOCK
        sem = ("arbitrary", "arbitrary")
        scratch = [pltpu.VMEM((META_BLOCK, HEAD_PAD), BF16)]
        b_axis, p_axis = 1, 0
    else:
        nq = S // tq
        grid = (B, 3, nq)
        ax = lambda f: f
        q_map = lambda b, p, t: ((g["main_off"] + b * S) // tq + t, p)
        o_map = lambda b, p, t: (b * nq + t, p)
        out_rows = B * S
        sem = ("arbitrary", "arbitrary", "arbitrary")
        scratch = []
        b_axis, p_axis = 0, 1
    in_specs = [
        pl.BlockSpec((tq, 2 * HEAD_PAD), ax(q_map)),
        pl.BlockSpec((S, kw), ax(lambda b, p, t: (main_blk + b, pk(p)))),
        pl.BlockSpec((META_BLOCK, kw), ax(lambda b, p, t: (meta_blk + b // per_blk, pk(p)))),
        pl.BlockSpec((n_tiles, vrows, TM), ax(lambda b, p, t: (main_blk + b, pk(p), 0))),
        pl.BlockSpec((1, vrows, META_BLOCK),
                     ax(lambda b, p, t: (g["meta_off"] // TM + (b * N_META) // TM, pk(p),
                                         ((b * N_META) % TM) // META_BLOCK))),
    ]
    unroll = None
    if n_chunks > ATTN_STATIC_CHUNKS:
        step = 2 * TM // ATTN_TK
        unroll = max(u for u in range(step, ATTN_UNROLL + 1, step) if n_chunks % u == 0)
        scratch = scratch + [pltpu.VMEM((2, ATTN_TK, tq), F32)]
    body = functools.partial(_attn_body, gqa=gqa, tq=tq, n_chunks=n_chunks, meta_q=meta_q,
                             b_axis=b_axis, p_axis=p_axis, unroll=unroll)
    return pl.pallas_call(
        body, grid=grid, in_specs=in_specs,
        out_specs=pl.BlockSpec((tq, HEAD_PAD), ax(o_map)),
        out_shape=jax.ShapeDtypeStruct((out_rows, 3 * HEAD_PAD), BF16),
        scratch_shapes=scratch, compiler_params=_params(sem),
        name=("gqa" if gqa else "mla") + ("_metaq_" if meta_q else "_") + g["name"],
    )(q, k, k, vt, vt)


def _gla_consts():
    c = GLA_CHUNK
    r = lax.broadcasted_iota(jnp.int32, (c, c), 0)
    s = lax.broadcasted_iota(jnp.int32, (c, c), 1)
    r4 = lax.broadcasted_iota(jnp.int32, (GLA_HEADS * c, c), 0) % c
    s4 = lax.broadcasted_iota(jnp.int32, (GLA_HEADS * c, c), 1)
    kl = lax.broadcasted_iota(jnp.int32, (1, GLA_HEADS * GLA_DK), 1) // GLA_DK
    vl = lax.broadcasted_iota(jnp.int32, (1, GLA_HEADS * GLA_DV), 1) // GLA_DV
    kr = lax.broadcasted_iota(jnp.int32, (GLA_HEADS * GLA_DK, GLA_HEADS * GLA_DV), 0) // GLA_DK
    vc = lax.broadcasted_iota(jnp.int32, (GLA_HEADS * GLA_DK, GLA_HEADS * GLA_DV), 1) // GLA_DV
    return dict(
        tril=(s <= r).astype(BF16), triu=(s >= r).astype(BF16),
        mask4_fw=s4 <= r4, mask4_bw=s4 >= r4,
        kmask=[(kl == h).astype(F32) for h in range(GLA_HEADS)],
        vmask=[(vl == h).astype(F32) for h in range(GLA_HEADS)],
        blockdiag=(kr == vc).astype(F32),
        ones=jnp.ones((c, GLA_HEADS * GLA_DV), BF16),
    )


def _gla_chunks(qs, ks, vs, gs, state, cst, fwd):
    c, n = GLA_CHUNK, len(qs)
    nk = GLA_HEADS * GLA_DK
    tri = cst["tril"] if fwd else cst["triu"]
    hilo = []
    for g in gs:
        hi = g.astype(BF16)
        hilo.append(jnp.concatenate([hi, (g - hi.astype(F32)).astype(BF16)], axis=1))
    bcum = [_dot(tri, x) for x in hilo]
    bcum = [x[:, :nk] + x[:, nk:] for x in bcum]
    ltot = [_dot_tn(x, cst["ones"]) for x in hilo]
    decay = [jnp.exp(x[:nk] + x[nk:]) for x in ltot]
    qe = [q * jnp.exp(b) for q, b in zip(qs, bcum)]
    ke = [(k * jnp.exp(-b)).astype(BF16) for k, b in zip(ks, bcum)]
    kd = [(k * jnp.exp((b[c - 1:c] if fwd else b[0:1]) - b)).astype(BF16) for k, b in zip(ks, bcum)]
    vb = [v.astype(BF16) for v in vs]
    qstack = [jnp.concatenate([x * cst["kmask"][h] for h in range(GLA_HEADS)], axis=0).astype(BF16) for x in qe]
    mask = cst["mask4_fw"] if fwd else cst["mask4_bw"]
    att = [jnp.where(mask, _dot_nt(x, y), 0.0).astype(BF16) for x, y in zip(qstack, ke)]
    full = [_dot(x, y) for x, y in zip(att, vb)]
    dstate = [_dot_tn(x, y) * cst["blockdiag"] for x, y in zip(kd, vb)]
    states = []
    for i in range(n):
        states.append(state.astype(BF16))
        state = decay[i] * state + dstate[i]
    outs = []
    for i in range(n):
        o = _dot(qe[i].astype(BF16), states[i])
        for h in range(GLA_HEADS):
            o = o + full[i][h * c:(h + 1) * c] * cst["vmask"][h]
        outs.append(o)
    return outs, state


def _meta_chunk(ref):
    x = ref[...]
    return jnp.concatenate([jnp.zeros((GLA_CHUNK - N_META, x.shape[1]), x.dtype), x], axis=0)


def _gla_fw_body(q_ref, k_ref, v_ref, g_ref, qm_ref, km_ref, vm_ref, gm_ref, o_ref, om_ref, state):
    cst = _gla_consts()

    @pl.when(pl.program_id(1) == 0)
    def _():
        (o,), s = _gla_chunks([_meta_chunk(qm_ref)], [_meta_chunk(km_ref)], [_meta_chunk(vm_ref)],
                              [_meta_chunk(gm_ref)], jnp.zeros(state.shape, F32), cst, True)
        om_ref[...] = o[GLA_CHUNK - N_META:]
        state[...] = s

    sls = [slice(c * GLA_CHUNK, (c + 1) * GLA_CHUNK) for c in range(TM // GLA_CHUNK)]
    outs, s = _gla_chunks([q_ref[sl, :] for sl in sls], [k_ref[sl, :] for sl in sls], [v_ref[sl, :] for sl in sls],
                          [g_ref[sl, :] for sl in sls], state[...], cst, True)
    for sl, o in zip(sls, outs):
        o_ref[sl, :] = o
    state[...] = s


def _gla_finish(o, gate, w, bd):
    x2 = o * o
    hi = x2.astype(BF16)
    lo = (x2 - hi.astype(F32)).astype(BF16)
    ms = _dot(hi, bd) + _dot(lo, bd)
    return (o * lax.rsqrt(ms + NORM_EPS) * w * (gate * jax.nn.sigmoid(gate))).astype(BF16)


def _gla_bw_body(q_ref, k_ref, v_ref, g_ref, ofw_ref, gate_ref, qm_ref, km_ref, vm_ref, gm_ref, ofwm_ref, gatem_ref,
                 w_ref, o_ref, om_ref, state):
    cst = _gla_consts()
    t = pl.program_id(1)
    r = lax.broadcasted_iota(jnp.int32, (GLA_HEADS * GLA_DV,) * 2, 0) // GLA_DV
    c_ = lax.broadcasted_iota(jnp.int32, (GLA_HEADS * GLA_DV,) * 2, 1) // GLA_DV
    bd = jnp.where(r == c_, 1.0 / GLA_DV, 0.0).astype(BF16)

    @pl.when(t == 0)
    def _():
        state[...] = jnp.zeros(state.shape, F32)

    sls = [slice(c * GLA_CHUNK, (c + 1) * GLA_CHUNK) for c in reversed(range(TM // GLA_CHUNK))]
    outs, s = _gla_chunks([q_ref[sl, :] for sl in sls], [k_ref[sl, :] for sl in sls], [v_ref[sl, :] for sl in sls],
                          [g_ref[sl, :] for sl in sls], state[...], cst, False)
    for sl, o in zip(sls, outs):
        o_ref[sl, :] = _gla_finish(o + ofw_ref[sl, :], gate_ref[sl, :], w_ref[...], bd)
    state[...] = s

    @pl.when(t == pl.num_programs(1) - 1)
    def _():
        (o,), _ = _gla_chunks([_meta_chunk(qm_ref)], [_meta_chunk(km_ref)], [_meta_chunk(vm_ref)],
                              [_meta_chunk(gm_ref)], s, cst, False)
        om_ref[...] = _gla_finish(o[GLA_CHUNK - N_META:] + ofwm_ref[...], gatem_ref[...], w_ref[...], bd)


def _gla_call(lq, lk, lv, gfw, gbw, lgate, wnorm, g):
    B, S = g["B"], g["S"]
    nt = S // TM
    main0 = g["main_off"] // TM
    meta0 = g["meta_off"] // N_META
    mrow = lambda w, rev: pl.BlockSpec(
        (TM, w), (lambda b, t: (main0 + b * nt + (nt - 1 - t), 0)) if rev else (lambda b, t: (main0 + b * nt + t, 0)))
    meta = lambda w: pl.BlockSpec((N_META, w), lambda b, t: (meta0 + b, 0))
    orow = lambda w, rev: pl.BlockSpec(
        (TM, w), (lambda b, t: (b * nt + (nt - 1 - t), 0)) if rev else (lambda b, t: (b * nt + t, 0)))
    ometa = lambda w: pl.BlockSpec((N_META, w), lambda b, t: (b, 0))
    sem = _params(("arbitrary", "arbitrary"))
    state = [pltpu.VMEM((GLA_HEADS * GLA_DK, GLA_HEADS * GLA_DV), F32)]
    ofw, ofw_meta = pl.pallas_call(
        _gla_fw_body, grid=(B, nt),
        in_specs=[mrow(128, False), mrow(128, False), mrow(256, False), mrow(128, False),
                  meta(128), meta(128), meta(256), meta(128)],
        out_specs=(orow(256, False), ometa(256)),
        out_shape=(jax.ShapeDtypeStruct((B * S, 256), F32), jax.ShapeDtypeStruct((B * N_META, 256), F32)),
        scratch_shapes=state, compiler_params=sem, name="gla_fw_" + g["name"],
    )(lq, lk, lv, gfw, lq, lk, lv, gfw)
    o, o_meta = pl.pallas_call(
        _gla_bw_body, grid=(B, nt),
        in_specs=[mrow(128, True), mrow(128, True), mrow(256, True), mrow(128, True), orow(256, True), mrow(256, True),
                  meta(128), meta(128), meta(256), meta(128), ometa(256), meta(256), _const_spec(wnorm.shape)],
        out_specs=(orow(256, True), ometa(256)),
        out_shape=(jax.ShapeDtypeStruct((B * S, 256), BF16), jax.ShapeDtypeStruct((B * N_META, 256), BF16)),
        scratch_shapes=state, compiler_params=sem, name="gla_bw_" + g["name"],
    )(lq, lk, lv, gbw, ofw, lgate, lq, lk, lv, gbw, ofw_meta, lgate, wnorm)
    return o, o_meta


def _rot_cols(w, n):
    s = w.shape
    w4 = w.reshape(s[:-1] + (s[-1] // (2 * n), 2, n))
    return jnp.stack([-w4[..., 1, :], w4[..., 0, :]], axis=-2).reshape(s)


def _swap_halves(w, n):
    w3 = w.reshape(-1, 2, n)
    return w3[:, ::-1, :].reshape(w.shape)


def _layout(bp, sp, bs, ss):
    up = lambda x: -(-x // TM) * TM
    lay = {"p": dict(name="p", B=bp, S=sp, main_off=0, main_rows=bp * sp),
           "s": dict(name="s", B=bs, S=ss, main_off=bp * sp, main_rows=bs * ss)}
    lay["p"]["meta_off"] = bp * sp + bs * ss
    lay["s"]["meta_off"] = lay["p"]["meta_off"] + up(bp * N_META)
    lay["R"] = lay["s"]["meta_off"] + up(bs * N_META)
    for g in (lay["p"], lay["s"]):
        assert g["S"] % TM == 0 and g["main_off"] % g["S"] == 0 and g["S"] % GRID_W == 0
    return lay


def _tile_table(lay):
    smax = max(lay["p"]["S"], lay["s"]["S"])
    tab = []
    for g in (lay["p"], lay["s"]):
        tab += [t % (g["S"] // TM) for t in range(g["main_rows"] // TM)]
    tab += [smax // TM] * ((lay["R"] - lay["p"]["meta_off"]) // TM)
    return jnp.asarray(np.asarray(tab, np.int32)), smax


def _rope_tables(smax):
    r = jnp.arange(smax, dtype=jnp.int32)
    meta = jnp.arange(TM, dtype=jnp.int32) % N_META
    zero = jnp.zeros((TM,), jnp.int32)
    pos = jnp.concatenate([r + N_META, meta]).astype(F32)
    row = jnp.concatenate([r // GRID_W, zero]).astype(F32)
    col = jnp.concatenate([r % GRID_W, zero]).astype(F32)
    inv_m = ROPE_THETA ** (-jnp.arange(0, MLA_ROPE, 2, dtype=F32) / MLA_ROPE)
    half = GQA_HEAD_DIM // 2
    inv_g = ROPE_THETA ** (-jnp.arange(0, half, 2, dtype=F32) / half)
    a1, ar, ac = pos[:, None] * inv_m, row[:, None] * inv_g, col[:, None] * inv_g
    n = pos.shape[0]
    one, zer = jnp.ones((n, MLA_NOPE), F32), jnp.zeros((n, MLA_NOPE), F32)
    pad = jnp.zeros((n, HEAD_PAD - MLA_NOPE - MLA_ROPE), F32)
    cm = jnp.concatenate([one, jnp.cos(a1), jnp.cos(a1), pad], axis=1)
    sm = jnp.concatenate([zer, jnp.sin(a1), jnp.sin(a1), pad], axis=1)
    cg = jnp.concatenate([jnp.cos(ar), jnp.cos(ar), jnp.cos(ac), jnp.cos(ac)] * 2, axis=1)
    sg = jnp.concatenate([jnp.sin(ar), jnp.sin(ar), jnp.sin(ac), jnp.sin(ac)] * 2, axis=1)
    return dict(cm=cm, sm=sm, cg=cg, sg=sg)


def _mixer_consts(i, mix_norm, w_in, mla_q_norm, mla_w_uq, mla_kv_norm, mla_w_ukv, gqa_q_norm, gqa_k_norm,
                  gate_fw_w, gate_fw_b, gate_bw_w, gate_bw_b):
    w = w_in[i]
    z = lambda n: jnp.zeros((D_MODEL, n), F32)
    o = np.cumsum([0, MLA_Q_LORA, MLA_KV_LORA, MLA_ROPE, 384, 128, 128, 128, 128, 256, 16, 16, 256])
    seg = lambda j: w[:, o[j]:o[j + 1]]
    w_kr, w_gq, w_gk = seg(2), seg(3), seg(4)
    w_gqr = _rot_cols(w_gq, 16)
    tail = HEAD_PAD - MLA_NOPE - MLA_ROPE

    def gq_blocks(m):
        out = []
        for h in range(GQA_HEADS):
            blk = m[:, h * GQA_HEAD_DIM:(h + 1) * GQA_HEAD_DIM]
            out += [blk, z(GQA_HEAD_DIM)] if h < GQA_HEADS // GQA_KV_HEADS else [z(GQA_HEAD_DIM), blk]
        return out

    wbig = jnp.concatenate(
        [seg(0), seg(1), z(MLA_NOPE), w_kr, z(tail), z(MLA_NOPE), _rot_cols(w_kr, 16), z(tail)]
        + gq_blocks(w_gq) + gq_blocks(w_gqr)
        + [w_gk, _rot_cols(w_gk, 16), seg(6), seg(7), seg(8), seg(11), seg(9), seg(10), z(128 - 2 * GLA_GATE_RANK)],
        axis=1).astype(BF16)
    assert wbig.shape[1] == C_END

    uq = mla_w_uq[i].reshape(MLA_Q_LORA, MLA_HEADS, MLA_NOPE + MLA_ROPE)
    zq = jnp.zeros((MLA_Q_LORA, MLA_HEADS, tail), F32)
    wqa = jnp.concatenate([uq, zq], axis=2).reshape(MLA_Q_LORA, -1).astype(BF16)
    wqb = jnp.concatenate([jnp.zeros((MLA_Q_LORA, MLA_HEADS, MLA_NOPE), F32),
                           _rot_cols(uq[:, :, MLA_NOPE:], 16), zq], axis=2).reshape(MLA_Q_LORA, -1).astype(BF16)
    ukv = mla_w_ukv[i].reshape(MLA_KV_LORA, MLA_HEADS, MLA_NOPE + MLA_V)
    wka = jnp.concatenate([ukv[:, :, :MLA_NOPE], jnp.zeros((MLA_KV_LORA, MLA_HEADS, HEAD_PAD - MLA_NOPE), F32)],
                          axis=2).reshape(MLA_KV_LORA, -1).astype(BF16)
    wvt = ukv[:, :, MLA_NOPE:].reshape(MLA_KV_LORA, -1).T.astype(BF16)
    wgvt = seg(5).T.astype(BF16)

    gq, gk = gqa_q_norm[i], gqa_k_norm[i]
    gqs, gks = _swap_halves(gq, 16), _swap_halves(gk, 16)
    two = lambda v: jnp.concatenate([v, v])[None, :]
    wg = jnp.zeros((128, 256), F32)
    wg = wg.at[:GLA_GATE_RANK, :128].set(gate_fw_w[i]).at[GLA_GATE_RANK:2 * GLA_GATE_RANK, 128:].set(gate_bw_w[i])
    return dict(
        nw=mix_norm[i][None, :], wbig=wbig, qn=mla_q_norm[i][None, :], kvn=mla_kv_norm[i][None, :],
        wqa=wqa, wqb=wqb, wka=wka, wvt=wvt, gqc=two(gq), gqs=two(gqs), gkc=two(gk), gks=two(gks),
        wgvt=wgvt, wg=wg.astype(BF16), bg=jnp.concatenate([gate_fw_b[i], gate_bw_b[i]])[None, :])


def _assemble(lay, parts):
    pieces = [parts["p"][0], parts["s"][0]]
    for name, nxt in (("p", lay["s"]["meta_off"]), ("s", lay["R"])):
        m = parts[name][1][:lay[name]["B"] * N_META]
        pad = nxt - lay[name]["meta_off"] - m.shape[0]
        pieces += [m, jnp.zeros((pad, m.shape[1]), m.dtype)]
    return jnp.concatenate(pieces, axis=0)


def kernel(x_prompt, x_sample, meta_tokens, final_norm, ffn1_norm, ffn1_w_in, ffn1_w_out, mix_norm, w_in, w_out,
           mla_q_norm, mla_w_uq, mla_kv_norm, mla_w_ukv, gqa_q_norm, gqa_k_norm, gla_gate_fw_w, gla_gate_fw_b,
           gla_gate_bw_w, gla_gate_bw_b, gla_out_norm, ffn2_norm, ffn2_w_in, ffn2_w_out):
    bp, sp, _ = x_prompt.shape
    bs, ss, _ = x_sample.shape
    depth = w_in.shape[0]
    lay = _layout(bp, sp, bs, ss)
    tile_tab, smax = _tile_table(lay)
    tabs = _rope_tables(smax)
    h = _assemble(lay, {"p": (x_prompt.reshape(-1, D_MODEL), jnp.tile(meta_tokens, (bp, 1))),
                        "s": (x_sample.reshape(-1, D_MODEL), jnp.tile(meta_tokens, (bs, 1)))})
    wnorm = jnp.tile(gla_out_norm, (1, GLA_HEADS))
    n_mla = MLA_HEADS * MLA_V
    tq = {"p": min(TM, sp), "s": min(TM, ss)}

    def ffn_w(w_i, w_o):
        return w_i[:, :D_FF].astype(BF16), w_i[:, D_FF:].astype(BF16), w_o.astype(BF16)

    y = None
    for i in range(depth):
        h = _ffn_call(h, ffn1_norm[i][None, :], *ffn_w(ffn1_w_in[i], ffn1_w_out[i]))
        consts = _mixer_consts(i, mix_norm, w_in, mla_q_norm, mla_w_uq, mla_kv_norm, mla_w_ukv, gqa_q_norm,
                               gqa_k_norm, gla_gate_fw_w, gla_gate_fw_b, gla_gate_bw_w, gla_gate_bw_b)
        qm, km, vmt, qg, kg, vgt, lq, lk, lv, gfw, gbw, lgate = _mixin_call(h, tile_tab, consts, tabs)
        o_mla, o_gqa, o_gla = {}, {}, {}
        for name in ("p", "s"):
            g = lay[name]
            o_mla[name] = (_attn_call(qm, km, vmt, g, gqa=False, meta_q=False, tq=tq[name]),
                           _attn_call(qm, km, vmt, g, gqa=False, meta_q=True, tq=None))
            o_gqa[name] = (_attn_call(qg, kg, vgt, g, gqa=True, meta_q=False, tq=tq[name]),
                           _attn_call(qg, kg, vgt, g, gqa=True, meta_q=True, tq=None))
            o_gla[name] = _gla_call(lq, lk, lv, gfw, gbw, lgate, wnorm[i][None, :], g)
        wo = w_out[i].astype(BF16)
        proj = (_assemble(lay, o_mla), _assemble(lay, o_gqa), _assemble(lay, o_gla),
                wo[:n_mla], wo[n_mla:2 * n_mla], wo[2 * n_mla:])
        last = i == depth - 1
        out = _ffn_call(h, ffn2_norm[i][None, :], *ffn_w(ffn2_w_in[i], ffn2_w_out[i]), proj=proj,
                        final=final_norm[None, :] if last else None, lay=lay)
        if last:
            y = out
        else:
            h = out
    return y[0].reshape(bp, sp, D_MODEL), y[1].reshape(bs, ss, D_MODEL)
```

```python
import functools

import numpy as np
import jax
import jax.numpy as jnp
from jax import lax
from jax.experimental import pallas as pl
from jax.experimental.pallas import tpu as pltpu

F32 = jnp.float32
BF16 = jnp.bfloat16

D_MODEL = 1024
N_META = 16
GRID_W = 64
NORM_EPS = 1e-6
ROPE_THETA = 10000.0
D_FF = 2816
MLA_HEADS = 6
MLA_Q_LORA = 256
MLA_KV_LORA = 128
MLA_NOPE = 64
MLA_ROPE = 32
MLA_V = 64
GQA_HEADS = 6
GQA_KV_HEADS = 2
GQA_HEAD_DIM = 64
GLA_HEADS = 4
GLA_DK = 32
GLA_DV = 64
GLA_GATE_RANK = 16
GLA_GATE_NORM = 16.0
GLA_CHUNK = 64

LANES = 128
TM = 512
META_BLOCK = 128
HEAD_PAD = 128
VMEM_LIMIT = 56 * 1024 * 1024
FF_SPLITS = (0, 1536, D_FF)
ATTN_TK = 512
ATTN_MAX_EXCESS = 64.0
ATTN_UNROLL = 16
ATTN_STATIC_CHUNKS = 4

LOG2E = 1.4426950408889634
MLA_SCALE = (MLA_NOPE + MLA_ROPE) ** -0.5 * LOG2E
GQA_SCALE = GQA_HEAD_DIM ** -0.5 * LOG2E
SUM_ROWS = 16
VROWS_MLA = SUM_ROWS + MLA_V
VROWS_GQA = GQA_KV_HEADS * (SUM_ROWS + GQA_HEAD_DIM)
GLA_QSCALE = GLA_DK ** -0.5

C_CQ = 0
C_CKV = C_CQ + MLA_Q_LORA
C_KP = C_CKV + MLA_KV_LORA
C_KPR = C_KP + HEAD_PAD
C_GQ = C_KPR + HEAD_PAD
C_GQR = C_GQ + GQA_HEADS * HEAD_PAD
C_GK = C_GQR + GQA_HEADS * HEAD_PAD
C_GKR = C_GK + 128
C_LQ = C_GKR + 128
C_LK = C_LQ + 128
C_LV = C_LK + 128
C_LGATE = C_LV + 256
C_LG = C_LGATE + 256
C_END = C_LG + 128

NT_DIMS = (((1,), (1,)), ((), ()))
TN_DIMS = (((0,), (0,)), ((), ()))


def _dot(a, b):
    return jnp.dot(a, b, preferred_element_type=F32)


def _dot_nt(a, b):
    return lax.dot_general(a, b, NT_DIMS, preferred_element_type=F32)


def _dot_tn(a, b):
    return lax.dot_general(a, b, TN_DIMS, preferred_element_type=F32)


def _rms(x, w):
    return x * lax.rsqrt(jnp.mean(x * x, axis=-1, keepdims=True) + NORM_EPS) * w


def _params(sem):
    return pltpu.CompilerParams(dimension_semantics=sem, vmem_limit_bytes=VMEM_LIMIT)


def _const_spec(shape):
    nd = len(shape)
    return pl.BlockSpec(shape, lambda *_: (0,) * nd, pipeline_mode=pl.Buffered(1))


def _ffn_body(*refs, split_src, has_proj, has_final, tiles_p, tiles_s):
    it = iter(refs)
    i = pl.program_id(0)
    if split_src:
        xp_ref, xs_ref, xm_ref = next(it), next(it), next(it)
        h = jnp.where(i < tiles_p, xp_ref[...], jnp.where(i < tiles_p + tiles_s, xs_ref[...], xm_ref[...]))
    else:
        h = next(it)[...]
    if has_proj:
        oa_ref, ob_ref, oc_ref, wa_ref, wb_ref, wc_ref = (next(it) for _ in range(6))
    nw_ref, wg_ref, wu_ref, wo_ref = (next(it) for _ in range(4))
    if has_final:
        fw_ref, yp_ref, ys_ref = next(it), next(it), next(it)
    else:
        out_ref = next(it)

    if has_proj:
        h = (h + _dot(oa_ref[...], wa_ref[...]) + _dot(ob_ref[...], wb_ref[...])
             + _dot(oc_ref[...], wc_ref[...]))
    xn = _rms(h, nw_ref[...]).astype(BF16)
    acc = jnp.zeros(h.shape, F32)
    for lo, hi in zip(FF_SPLITS[:-1], FF_SPLITS[1:]):
        g = _dot(xn, wg_ref[:, lo:hi])
        u = _dot(xn, wu_ref[:, lo:hi])
        a = (g * jax.nn.sigmoid(g) * u).astype(BF16)
        acc = acc + _dot(a, wo_ref[lo:hi, :])
    h2 = h + 0.5 * acc
    if has_final:
        y = _rms(h2, fw_ref[...])

        @pl.when(i < tiles_p)
        def _():
            yp_ref[...] = y

        @pl.when(jnp.logical_and(i >= tiles_p, i < tiles_p + tiles_s))
        def _():
            ys_ref[...] = y
    else:
        out_ref[...] = h2


def _ffn_call(h, nw, wg, wu, wo, lay, proj=None, final=None):
    R = lay["R"]
    n_tiles = R // TM
    tiles_p, tiles_s = lay["p"]["main_rows"] // TM, lay["s"]["main_rows"] // TM
    row = lambda w: pl.BlockSpec((TM, w), lambda i: (i, 0))
    part = lambda first, count: pl.BlockSpec((TM, D_MODEL), lambda i: (jnp.clip(i - first, 0, count - 1), 0))
    split_src = isinstance(h, tuple)
    if split_src:
        ins = list(h)
        specs = [part(0, tiles_p), part(tiles_p, tiles_s), part(tiles_p + tiles_s, n_tiles - tiles_p - tiles_s)]
    else:
        ins, specs = [h], [row(D_MODEL)]
    if proj is not None:
        oa, ob, oc, wa, wb, wc = proj
        ins += [oa, ob, oc, wa, wb, wc]
        specs += [row(oa.shape[1]), row(ob.shape[1]), row(oc.shape[1]),
                  _const_spec(wa.shape), _const_spec(wb.shape), _const_spec(wc.shape)]
    ins += [nw, wg, wu, wo]
    specs += [_const_spec(nw.shape), _const_spec(wg.shape), _const_spec(wu.shape), _const_spec(wo.shape)]
    if final is not None:
        ins.append(final)
        specs.append(_const_spec(final.shape))
        out_shape = (jax.ShapeDtypeStruct((tiles_p * TM, D_MODEL), F32),
                     jax.ShapeDtypeStruct((tiles_s * TM, D_MODEL), F32))
        out_specs = (part(0, tiles_p), part(tiles_p, tiles_s))
    else:
        out_shape = jax.ShapeDtypeStruct((R, D_MODEL), F32)
        out_specs = row(D_MODEL)
    body = functools.partial(_ffn_body, split_src=split_src, has_proj=proj is not None, has_final=final is not None,
                             tiles_p=tiles_p, tiles_s=tiles_s)
    return pl.pallas_call(
        body, grid=(n_tiles,), in_specs=specs, out_specs=out_specs, out_shape=out_shape,
        compiler_params=_params(("arbitrary",)),
        name="ffn" + ("_proj" if proj is not None else "") + ("_final" if final is not None else ""),
    )(*ins)


def _mixin_body(tab_ref, h_ref, nw_ref, wbig_ref, cm_ref, sm_ref, cg_ref, sg_ref,
                qn_ref, kvn_ref, wqa_ref, wqb_ref, wka_ref, wvt_ref,
                gqc_ref, gqs_ref, gkc_ref, gks_ref, wgvt_ref, wg_ref, bg_ref,
                qm_ref, km_ref, vmt_ref, qg_ref, kg_ref, vgt_ref,
                lq_ref, lk_ref, lv_ref, gfw_ref, gbw_ref, lgate_ref):
    del tab_ref
    hn = _rms(h_ref[...], nw_ref[...]).astype(BF16)

    pall = _dot(hn, wbig_ref[...])

    def proj(a, width):
        return pall[:, a:a + width]

    cm, sm = cm_ref[...], sm_ref[...]
    cg, sg = cg_ref[...], sg_ref[...]

    cqn = _rms(proj(C_CQ, MLA_Q_LORA), qn_ref[...]).astype(BF16)
    qa = _dot(cqn, wqa_ref[...])
    qb = _dot(cqn, wqb_ref[...])
    for hd in range(MLA_HEADS):
        blk = slice(hd * HEAD_PAD, (hd + 1) * HEAD_PAD)
        qm_ref[:, blk] = ((qa[:, blk] * cm + qb[:, blk] * sm) * MLA_SCALE).astype(BF16)

    kvn = _rms(proj(C_CKV, MLA_KV_LORA), kvn_ref[...]).astype(BF16)
    kpe = proj(C_KP, HEAD_PAD) * cm + proj(C_KPR, HEAD_PAD) * sm
    ka = _dot(kvn, wka_ref[...])
    for hd in range(MLA_HEADS):
        blk = slice(hd * HEAD_PAD, (hd + 1) * HEAD_PAD)
        km_ref[:, blk] = (ka[:, blk] + kpe).astype(BF16)
    ones = jnp.ones((SUM_ROWS, hn.shape[0]), F32)
    vt = _dot_nt(wvt_ref[...], kvn)
    pieces = []
    for hd in range(MLA_HEADS):
        pieces += [ones, vt[hd * MLA_V:(hd + 1) * MLA_V]]
    vmt_ref[0] = jnp.concatenate(pieces, axis=0).astype(BF16)

    tqc, tqs = gqc_ref[...] * cg, gqs_ref[...] * sg
    for hd in range(GQA_HEADS):
        x = proj(C_GQ + hd * HEAD_PAD, HEAD_PAD)
        xr = proj(C_GQR + hd * HEAD_PAD, HEAD_PAD)
        n = lax.rsqrt(jnp.sum(x * x, axis=-1, keepdims=True) * (1.0 / GQA_HEAD_DIM) + NORM_EPS)
        qg_ref[:, hd * HEAD_PAD:(hd + 1) * HEAD_PAD] = (n * (x * tqc + xr * tqs) * GQA_SCALE).astype(BF16)
    x = proj(C_GK, 128)
    xr = proj(C_GKR, 128)
    lo = lax.broadcasted_iota(jnp.int32, x.shape, 1) < GQA_HEAD_DIM
    x2 = x * x
    ms0 = jnp.sum(jnp.where(lo, x2, 0.0), axis=-1, keepdims=True) * (1.0 / GQA_HEAD_DIM)
    ms1 = jnp.sum(jnp.where(lo, 0.0, x2), axis=-1, keepdims=True) * (1.0 / GQA_HEAD_DIM)
    n = jnp.where(lo, lax.rsqrt(ms0 + NORM_EPS), lax.rsqrt(ms1 + NORM_EPS))
    kg_ref[...] = (n * (x * (gkc_ref[...] * cg) + xr * (gks_ref[...] * sg))).astype(BF16)
    vg = _dot_nt(wgvt_ref[...], hn)
    vgt_ref[0] = jnp.concatenate([ones, vg[:GQA_HEAD_DIM], ones, vg[GQA_HEAD_DIM:]], axis=0).astype(BF16)

    lq_ref[...] = proj(C_LQ, 128) * GLA_QSCALE
    lk_ref[...] = proj(C_LK, 128)
    lv_ref[...] = proj(C_LV, 256)
    lgate_ref[...] = proj(C_LGATE, 256)
    gg = _dot(proj(C_LG, 128).astype(BF16), wg_ref[...]) + bg_ref[...]
    ls = (jnp.minimum(gg, 0.0) - jnp.log1p(jnp.exp(-jnp.abs(gg)))) * (1.0 / GLA_GATE_NORM)
    gfw_ref[...] = ls[:, :128]
    gbw_ref[...] = ls[:, 128:]


def _mixin_call(h, tile_tab, consts, tabs):
    R = h.shape[0]
    n_tiles = R // TM
    row = lambda w: pl.BlockSpec((TM, w), lambda i, t: (i, 0))
    tabspec = pl.BlockSpec((TM, LANES), lambda i, t: (t[i], 0))
    cs = lambda a: _const_spec(a.shape)
    col3 = lambda r: pl.BlockSpec((1, r, TM), lambda i, t: (i, 0, 0))
    ins = [h, consts["nw"], consts["wbig"], tabs["cm"], tabs["sm"], tabs["cg"], tabs["sg"],
           consts["qn"], consts["kvn"], consts["wqa"], consts["wqb"], consts["wka"], consts["wvt"],
           consts["gqc"], consts["gqs"], consts["gkc"], consts["gks"], consts["wgvt"],
           consts["wg"], consts["bg"]]
    specs = [row(D_MODEL), cs(ins[1]), cs(ins[2]), tabspec, tabspec, tabspec, tabspec] + [cs(a) for a in ins[7:]]
    W6 = MLA_HEADS * HEAD_PAD
    out_shape = (
        jax.ShapeDtypeStruct((R, W6), BF16), jax.ShapeDtypeStruct((R, W6), BF16),
        jax.ShapeDtypeStruct((n_tiles, MLA_HEADS * VROWS_MLA, TM), BF16),
        jax.ShapeDtypeStruct((R, W6), BF16), jax.ShapeDtypeStruct((R, 128), BF16),
        jax.ShapeDtypeStruct((n_tiles, VROWS_GQA, TM), BF16),
        jax.ShapeDtypeStruct((R, 128), F32), jax.ShapeDtypeStruct((R, 128), F32),
        jax.ShapeDtypeStruct((R, 256), F32), jax.ShapeDtypeStruct((R, 128), F32),
        jax.ShapeDtypeStruct((R, 128), F32), jax.ShapeDtypeStruct((R, 256), F32),
    )
    out_specs = (row(W6), row(W6), col3(MLA_HEADS * VROWS_MLA), row(W6), row(128), col3(VROWS_GQA),
                 row(128), row(128), row(256), row(128), row(128), row(256))
    gs = pltpu.PrefetchScalarGridSpec(num_scalar_prefetch=1, grid=(n_tiles,), in_specs=specs, out_specs=out_specs)
    return pl.pallas_call(_mixin_body, grid_spec=gs, out_shape=out_shape,
                          compiler_params=_params(("arbitrary",)), name="mixer_in")(tile_tab, *ins)


def _attn_body(q_ref, k_ref, kmeta_ref, vt_ref, vtmeta_ref, o_ref, *scratch,
               gqa, tq, n_chunks, meta_q, b_axis, p_axis, unroll):
    b = pl.program_id(b_axis)
    p = pl.program_id(p_axis)
    slot = b % (META_BLOCK // N_META)
    rows = lax.broadcasted_iota(jnp.int32, (META_BLOCK, 1), 0)
    meta_valid = (rows // N_META) == slot
    zeros = jnp.zeros((VROWS_MLA, tq), F32)

    def exact_update(carry, s_t, v_t):
        m, acc = carry
        m_new = jnp.maximum(m, jnp.max(s_t, axis=0, keepdims=True))
        p_t = jnp.exp2(s_t - m_new).astype(BF16)
        return m_new, jnp.exp2(m - m_new) * acc + _dot(v_t, p_t)

    def fast_update(carry, s_t, v_t):
        m, acc, excess = carry
        p_t = jnp.exp2(s_t - m).astype(BF16)
        cmax = jnp.max(s_t, axis=0, keepdims=True)
        m_new = jnp.maximum(m, cmax)
        return m_new, jnp.exp2(m - m_new) * (acc + _dot(v_t, p_t)), jnp.maximum(excess, cmax - m)

    def heads(fast):
        update = fast_update if fast else exact_update
        parts = TM // ATTN_TK
        q = [q_ref[:, i * HEAD_PAD:(i + 1) * HEAD_PAD] for i in range(2)]
        ks = [slice(0, HEAD_PAD) if gqa else slice(i * HEAD_PAD, (i + 1) * HEAD_PAD) for i in range(2)]
        if gqa:
            groups = [(2 * p + i) // (GQA_HEADS // GQA_KV_HEADS) for i in range(2)]
            vs = [pl.ds(pl.multiple_of(g * VROWS_MLA, SUM_ROWS), VROWS_MLA) for g in groups]
        else:
            vs = [pl.ds(i * VROWS_MLA, VROWS_MLA) for i in range(2)]

        def scores(i, c):
            return _dot_nt(k_ref[pl.ds(pl.multiple_of(c * ATTN_TK, ATTN_TK), ATTN_TK), ks[i]], q[i])

        def values(i, tile, part):
            return vt_ref[tile, vs[i], part * ATTN_TK:(part + 1) * ATTN_TK]

        def meta_scores(i):
            return jnp.where(meta_valid, _dot_nt(kmeta_ref[:, ks[i]], q[i]), -jnp.inf)

        def init(s0):
            if fast:
                return jnp.max(s0, axis=0, keepdims=True), zeros, jnp.zeros((1, tq), F32)
            return jnp.full((1, tq), -jnp.inf, F32), zeros

        if n_chunks <= ATTN_STATIC_CHUNKS:
            out = []
            for i in range(2):
                s_cur = scores(i, 0)
                carry = init(s_cur)
                for c in range(n_chunks):
                    s_next = scores(i, c + 1) if c + 1 < n_chunks else meta_scores(i)
                    carry = update(carry, s_cur, values(i, c // parts, c % parts))
                    s_cur = s_next
                out.append(update(carry, s_cur, vtmeta_ref[0, vs[i], :]))
            return out

        s_buf = scratch[-1]
        carry = []
        for i in range(2):
            s0 = scores(i, 0)
            s_buf[i, 0] = s0
            carry.append(init(s0))

        def group_step(j, st):
            st = list(st)
            for u in range(unroll):
                c = j * unroll + u
                for i in range(2):
                    s_buf[i, (u + 1) % 2] = scores(i, jnp.minimum(c + 1, n_chunks - 1))
                    st[i] = update(st[i], s_buf[i, u % 2], values(i, j * (unroll // parts) + u // parts, u % parts))
            return tuple(st)

        carry = lax.fori_loop(0, n_chunks // unroll, group_step, tuple(carry))
        return [update(carry[i], meta_scores(i), vtmeta_ref[0, vs[i], :]) for i in range(2)]

    if n_chunks <= ATTN_STATIC_CHUNKS:
        fast = heads(True)
        worst = jnp.max(jnp.maximum(fast[0][2], fast[1][2]))
        accs = lax.cond(worst > ATTN_MAX_EXCESS,
                        lambda: tuple(c[1] for c in heads(False)),
                        lambda: tuple(c[1] for c in fast))
    else:
        accs = [c[1] for c in heads(False)]
    halves = [acc[SUM_ROWS:] / acc[0:1] for acc in accs]
    o = jnp.concatenate(halves, axis=0).T.astype(BF16)
    if meta_q:
        stage = scratch[0]

        @pl.when(slot == 0)
        def _():
            o_ref[...] = jnp.zeros(o_ref.shape, o_ref.dtype)

        stage[...] = o
        off = pl.multiple_of(slot * N_META, N_META)
        o_ref[pl.ds(off, N_META), :] = stage[pl.ds(off, N_META), :]
    else:
        o_ref[...] = o


def _attn_call(q, k, vt, g, *, gqa, meta_q, tq):
    B, S = g["B"], g["S"]
    n_chunks = S // ATTN_TK
    n_tiles = S // TM
    kw = HEAD_PAD if gqa else 2 * HEAD_PAD
    vrows = VROWS_GQA if gqa else 2 * VROWS_MLA
    main_blk = g["main_off"] // S
    meta_blk = g["meta_off"] // META_BLOCK
    per_blk = META_BLOCK // N_META
    pk = (lambda p: 0) if gqa else (lambda p: p)
    if meta_q:
        grid = (3, B)
        ax = lambda f: (lambda p, b: f(b, p, 0))
        tq = META_BLOCK
        q_map = lambda b, p, t: (meta_blk + b // per_blk, p)
        o_map = lambda b, p, t: (b // per_blk, p)
        out_rows = pl.cdiv(B * N_META, META_BLOCK) * META_BLOCK
        sem = ("arbitrary", "arbitrary")
        scratch = [pltpu.VMEM((META_BLOCK, HEAD_PAD), BF16)]
        b_axis, p_axis = 1, 0
    else:
        nq = S // tq
        grid = (B, 3, nq)
        ax = lambda f: f
        q_map = lambda b, p, t: ((g["main_off"] + b * S) // tq + t, p)
        o_map = lambda b, p, t: (b * nq + t, p)
        out_rows = B * S
        sem = ("arbitrary", "arbitrary", "arbitrary")
        scratch = []
        b_axis, p_axis = 0, 1
    in_specs = [
        pl.BlockSpec((tq, 2 * HEAD_PAD), ax(q_map)),
        pl.BlockSpec((S, kw), ax(lambda b, p, t: (main_blk + b, pk(p)))),
        pl.BlockSpec((META_BLOCK, kw), ax(lambda b, p, t: (meta_blk + b // per_blk, pk(p)))),
        pl.BlockSpec((n_tiles, vrows, TM), ax(lambda b, p, t: (main_blk + b, pk(p), 0))),
        pl.BlockSpec((1, vrows, META_BLOCK),
                     ax(lambda b, p, t: (g["meta_off"] // TM + (b * N_META) // TM, pk(p),
                                         ((b * N_META) % TM) // META_BLOCK))),
    ]
    unroll = None
    if n_chunks > ATTN_STATIC_CHUNKS:
        step = 2 * TM // ATTN_TK
        unroll = max(u for u in range(step, ATTN_UNROLL + 1, step) if n_chunks % u == 0)
        scratch = scratch + [pltpu.VMEM((2, 2, ATTN_TK, tq), F32)]
    body = functools.partial(_attn_body, gqa=gqa, tq=tq, n_chunks=n_chunks, meta_q=meta_q,
                             b_axis=b_axis, p_axis=p_axis, unroll=unroll)
    return pl.pallas_call(
        body, grid=grid, in_specs=in_specs,
        out_specs=pl.BlockSpec((tq, HEAD_PAD), ax(o_map)),
        out_shape=jax.ShapeDtypeStruct((out_rows, 3 * HEAD_PAD), BF16),
        scratch_shapes=scratch, compiler_params=_params(sem),
        name=("gqa" if gqa else "mla") + ("_metaq_" if meta_q else "_") + g["name"],
    )(q, k, k, vt, vt)


def _gla_consts():
    c = GLA_CHUNK
    r = lax.broadcasted_iota(jnp.int32, (c, c), 0)
    s = lax.broadcasted_iota(jnp.int32, (c, c), 1)
    r4 = lax.broadcasted_iota(jnp.int32, (GLA_HEADS * c, c), 0) % c
    s4 = lax.broadcasted_iota(jnp.int32, (GLA_HEADS * c, c), 1)
    kl = lax.broadcasted_iota(jnp.int32, (1, GLA_HEADS * GLA_DK), 1) // GLA_DK
    vl = lax.broadcasted_iota(jnp.int32, (1, GLA_HEADS * GLA_DV), 1) // GLA_DV
    kr = lax.broadcasted_iota(jnp.int32, (GLA_HEADS * GLA_DK, GLA_HEADS * GLA_DV), 0) // GLA_DK
    vc = lax.broadcasted_iota(jnp.int32, (GLA_HEADS * GLA_DK, GLA_HEADS * GLA_DV), 1) // GLA_DV
    return dict(
        tril=(s <= r).astype(BF16), triu=(s >= r).astype(BF16),
        mask4_fw=s4 <= r4, mask4_bw=s4 >= r4,
        kmask=[(kl == h).astype(F32) for h in range(GLA_HEADS)],
        vmask=[(vl == h).astype(F32) for h in range(GLA_HEADS)],
        blockdiag=(kr == vc).astype(F32),
        ones=jnp.ones((c, GLA_HEADS * GLA_DV), BF16),
    )


def _gla_chunks(qs, ks, vs, gs, state, cst, fwd):
    c, n = GLA_CHUNK, len(qs)
    nk = GLA_HEADS * GLA_DK
    tri = cst["tril"] if fwd else cst["triu"]
    hilo = []
    for g in gs:
        hi = g.astype(BF16)
        hilo.append(jnp.concatenate([hi, (g - hi.astype(F32)).astype(BF16)], axis=1))
    bcum = [_dot(tri, x) for x in hilo]
    bcum = [x[:, :nk] + x[:, nk:] for x in bcum]
    ltot = [_dot_tn(x, cst["ones"]) for x in hilo]
    decay = [jnp.exp(x[:nk] + x[nk:]) for x in ltot]
    qe = [q * jnp.exp(b) for q, b in zip(qs, bcum)]
    ke = [(k * jnp.exp(-b)).astype(BF16) for k, b in zip(ks, bcum)]
    kd = [(k * jnp.exp((b[c - 1:c] if fwd else b[0:1]) - b)).astype(BF16) for k, b in zip(ks, bcum)]
    vb = [v.astype(BF16) for v in vs]
    qstack = [jnp.concatenate([x * cst["kmask"][h] for h in range(GLA_HEADS)], axis=0).astype(BF16) for x in qe]
    mask = cst["mask4_fw"] if fwd else cst["mask4_bw"]
    att = [jnp.where(mask, _dot_nt(x, y), 0.0).astype(BF16) for x, y in zip(qstack, ke)]
    full = [_dot(x, y) for x, y in zip(att, vb)]
    dstate = [_dot_tn(x, y) * cst["blockdiag"] for x, y in zip(kd, vb)]
    states = []
    for i in range(n):
        states.append(state.astype(BF16))
        state = decay[i] * state + dstate[i]
    outs = []
    for i in range(n):
        o = _dot(qe[i].astype(BF16), states[i])
        for h in range(GLA_HEADS):
            o = o + full[i][h * c:(h + 1) * c] * cst["vmask"][h]
        outs.append(o)
    return outs, state


def _meta_chunk(ref):
    x = ref[...]
    return jnp.concatenate([jnp.zeros((GLA_CHUNK - N_META, x.shape[1]), x.dtype), x], axis=0)


def _gla_fw_body(q_ref, k_ref, v_ref, g_ref, qm_ref, km_ref, vm_ref, gm_ref, o_ref, om_ref, state):
    cst = _gla_consts()

    @pl.when(pl.program_id(1) == 0)
    def _():
        (o,), s = _gla_chunks([_meta_chunk(qm_ref)], [_meta_chunk(km_ref)], [_meta_chunk(vm_ref)],
                              [_meta_chunk(gm_ref)], jnp.zeros(state.shape, F32), cst, True)
        om_ref[...] = o[GLA_CHUNK - N_META:]
        state[...] = s

    sls = [slice(c * GLA_CHUNK, (c + 1) * GLA_CHUNK) for c in range(TM // GLA_CHUNK)]
    outs, s = _gla_chunks([q_ref[sl, :] for sl in sls], [k_ref[sl, :] for sl in sls], [v_ref[sl, :] for sl in sls],
                          [g_ref[sl, :] for sl in sls], state[...], cst, True)
    for sl, o in zip(sls, outs):
        o_ref[sl, :] = o
    state[...] = s


def _gla_finish(o, gate, w, bd):
    x2 = o * o
    hi = x2.astype(BF16)
    lo = (x2 - hi.astype(F32)).astype(BF16)
    ms = _dot(hi, bd) + _dot(lo, bd)
    return (o * lax.rsqrt(ms + NORM_EPS) * w * (gate * jax.nn.sigmoid(gate))).astype(BF16)


def _gla_bw_body(q_ref, k_ref, v_ref, g_ref, ofw_ref, gate_ref, qm_ref, km_ref, vm_ref, gm_ref, ofwm_ref, gatem_ref,
                 w_ref, o_ref, om_ref, state):
    cst = _gla_consts()
    t = pl.program_id(1)
    r = lax.broadcasted_iota(jnp.int32, (GLA_HEADS * GLA_DV,) * 2, 0) // GLA_DV
    c_ = lax.broadcasted_iota(jnp.int32, (GLA_HEADS * GLA_DV,) * 2, 1) // GLA_DV
    bd = jnp.where(r == c_, 1.0 / GLA_DV, 0.0).astype(BF16)

    @pl.when(t == 0)
    def _():
        state[...] = jnp.zeros(state.shape, F32)

    sls = [slice(c * GLA_CHUNK, (c + 1) * GLA_CHUNK) for c in reversed(range(TM // GLA_CHUNK))]
    outs, s = _gla_chunks([q_ref[sl, :] for sl in sls], [k_ref[sl, :] for sl in sls], [v_ref[sl, :] for sl in sls],
                          [g_ref[sl, :] for sl in sls], state[...], cst, False)
    for sl, o in zip(sls, outs):
        o_ref[sl, :] = _gla_finish(o + ofw_ref[sl, :], gate_ref[sl, :], w_ref[...], bd)
    state[...] = s

    @pl.when(t == pl.num_programs(1) - 1)
    def _():
        (o,), _ = _gla_chunks([_meta_chunk(qm_ref)], [_meta_chunk(km_ref)], [_meta_chunk(vm_ref)],
                              [_meta_chunk(gm_ref)], s, cst, False)
        om_ref[...] = _gla_finish(o[GLA_CHUNK - N_META:] + ofwm_ref[...], gatem_ref[...], w_ref[...], bd)


def _gla_call(lq, lk, lv, gfw, gbw, lgate, wnorm, g):
    B, S = g["B"], g["S"]
    nt = S // TM
    main0 = g["main_off"] // TM
    meta0 = g["meta_off"] // N_META
    mrow = lambda w, rev: pl.BlockSpec(
        (TM, w), (lambda b, t: (main0 + b * nt + (nt - 1 - t), 0)) if rev else (lambda b, t: (main0 + b * nt + t, 0)))
    meta = lambda w: pl.BlockSpec((N_META, w), lambda b, t: (meta0 + b, 0))
    orow = lambda w, rev: pl.BlockSpec(
        (TM, w), (lambda b, t: (b * nt + (nt - 1 - t), 0)) if rev else (lambda b, t: (b * nt + t, 0)))
    ometa = lambda w: pl.BlockSpec((N_META, w), lambda b, t: (b, 0))
    sem = _params(("arbitrary", "arbitrary"))
    state = [pltpu.VMEM((GLA_HEADS * GLA_DK, GLA_HEADS * GLA_DV), F32)]
    ofw, ofw_meta = pl.pallas_call(
        _gla_fw_body, grid=(B, nt),
        in_specs=[mrow(128, False), mrow(128, False), mrow(256, False), mrow(128, False),
                  meta(128), meta(128), meta(256), meta(128)],
        out_specs=(orow(256, False), ometa(256)),
        out_shape=(jax.ShapeDtypeStruct((B * S, 256), F32), jax.ShapeDtypeStruct((B * N_META, 256), F32)),
        scratch_shapes=state, compiler_params=sem, name="gla_fw_" + g["name"],
    )(lq, lk, lv, gfw, lq, lk, lv, gfw)
    o, o_meta = pl.pallas_call(
        _gla_bw_body, grid=(B, nt),
        in_specs=[mrow(128, True), mrow(128, True), mrow(256, True), mrow(128, True), orow(256, True), mrow(256, True),
                  meta(128), meta(128), meta(256), meta(128), ometa(256), meta(256), _const_spec(wnorm.shape)],
        out_specs=(orow(256, True), ometa(256)),
        out_shape=(jax.ShapeDtypeStruct((B * S, 256), BF16), jax.ShapeDtypeStruct((B * N_META, 256), BF16)),
        scratch_shapes=state, compiler_params=sem, name="gla_bw_" + g["name"],
    )(lq, lk, lv, gbw, ofw, lgate, lq, lk, lv, gbw, ofw_meta, lgate, wnorm)
    return o, o_meta


def _rot_cols(w, n):
    s = w.shape
    w4 = w.reshape(s[:-1] + (s[-1] // (2 * n), 2, n))
    return jnp.stack([-w4[..., 1, :], w4[..., 0, :]], axis=-2).reshape(s)


def _swap_halves(w, n):
    w3 = w.reshape(-1, 2, n)
    return w3[:, ::-1, :].reshape(w.shape)


def _layout(bp, sp, bs, ss):
    up = lambda x: -(-x // TM) * TM
    lay = {"p": dict(name="p", B=bp, S=sp, main_off=0, main_rows=bp * sp),
           "s": dict(name="s", B=bs, S=ss, main_off=bp * sp, main_rows=bs * ss)}
    lay["p"]["meta_off"] = bp * sp + bs * ss
    lay["s"]["meta_off"] = lay["p"]["meta_off"] + up(bp * N_META)
    lay["R"] = lay["s"]["meta_off"] + up(bs * N_META)
    for g in (lay["p"], lay["s"]):
        assert g["S"] % TM == 0 and g["main_off"] % g["S"] == 0 and g["S"] % GRID_W == 0
    return lay


def _tile_table(lay):
    smax = max(lay["p"]["S"], lay["s"]["S"])
    tab = []
    for g in (lay["p"], lay["s"]):
        tab += [t % (g["S"] // TM) for t in range(g["main_rows"] // TM)]
    tab += [smax // TM] * ((lay["R"] - lay["p"]["meta_off"]) // TM)
    return jnp.asarray(np.asarray(tab, np.int32)), smax


def _rope_tables(smax):
    r = jnp.arange(smax, dtype=jnp.int32)
    meta = jnp.arange(TM, dtype=jnp.int32) % N_META
    zero = jnp.zeros((TM,), jnp.int32)
    pos = jnp.concatenate([r + N_META, meta]).astype(F32)
    row = jnp.concatenate([r // GRID_W, zero]).astype(F32)
    col = jnp.concatenate([r % GRID_W, zero]).astype(F32)
    inv_m = ROPE_THETA ** (-jnp.arange(0, MLA_ROPE, 2, dtype=F32) / MLA_ROPE)
    half = GQA_HEAD_DIM // 2
    inv_g = ROPE_THETA ** (-jnp.arange(0, half, 2, dtype=F32) / half)
    a1, ar, ac = pos[:, None] * inv_m, row[:, None] * inv_g, col[:, None] * inv_g
    n = pos.shape[0]
    one, zer = jnp.ones((n, MLA_NOPE), F32), jnp.zeros((n, MLA_NOPE), F32)
    pad = jnp.zeros((n, HEAD_PAD - MLA_NOPE - MLA_ROPE), F32)
    cm = jnp.concatenate([one, jnp.cos(a1), jnp.cos(a1), pad], axis=1)
    sm = jnp.concatenate([zer, jnp.sin(a1), jnp.sin(a1), pad], axis=1)
    cg = jnp.concatenate([jnp.cos(ar), jnp.cos(ar), jnp.cos(ac), jnp.cos(ac)] * 2, axis=1)
    sg = jnp.concatenate([jnp.sin(ar), jnp.sin(ar), jnp.sin(ac), jnp.sin(ac)] * 2, axis=1)
    return dict(cm=cm, sm=sm, cg=cg, sg=sg)


def _mixer_consts(i, mix_norm, w_in, mla_q_norm, mla_w_uq, mla_kv_norm, mla_w_ukv, gqa_q_norm, gqa_k_norm,
                  gate_fw_w, gate_fw_b, gate_bw_w, gate_bw_b):
    w = w_in[i]
    z = lambda n: jnp.zeros((D_MODEL, n), F32)
    o = np.cumsum([0, MLA_Q_LORA, MLA_KV_LORA, MLA_ROPE, 384, 128, 128, 128, 128, 256, 16, 16, 256])
    seg = lambda j: w[:, o[j]:o[j + 1]]
    w_kr, w_gq, w_gk = seg(2), seg(3), seg(4)
    w_gqr = _rot_cols(w_gq, 16)
    tail = HEAD_PAD - MLA_NOPE - MLA_ROPE

    def gq_blocks(m):
        out = []
        for h in range(GQA_HEADS):
            blk = m[:, h * GQA_HEAD_DIM:(h + 1) * GQA_HEAD_DIM]
            out += [blk, z(GQA_HEAD_DIM)] if h < GQA_HEADS // GQA_KV_HEADS else [z(GQA_HEAD_DIM), blk]
        return out

    wbig = jnp.concatenate(
        [seg(0), seg(1), z(MLA_NOPE), w_kr, z(tail), z(MLA_NOPE), _rot_cols(w_kr, 16), z(tail)]
        + gq_blocks(w_gq) + gq_blocks(w_gqr)
        + [w_gk, _rot_cols(w_gk, 16), seg(6), seg(7), seg(8), seg(11), seg(9), seg(10), z(128 - 2 * GLA_GATE_RANK)],
        axis=1).astype(BF16)
    assert wbig.shape[1] == C_END

    uq = mla_w_uq[i].reshape(MLA_Q_LORA, MLA_HEADS, MLA_NOPE + MLA_ROPE)
    zq = jnp.zeros((MLA_Q_LORA, MLA_HEADS, tail), F32)
    wqa = jnp.concatenate([uq, zq], axis=2).reshape(MLA_Q_LORA, -1).astype(BF16)
    wqb = jnp.concatenate([jnp.zeros((MLA_Q_LORA, MLA_HEADS, MLA_NOPE), F32),
                           _rot_cols(uq[:, :, MLA_NOPE:], 16), zq], axis=2).reshape(MLA_Q_LORA, -1).astype(BF16)
    ukv = mla_w_ukv[i].reshape(MLA_KV_LORA, MLA_HEADS, MLA_NOPE + MLA_V)
    wka = jnp.concatenate([ukv[:, :, :MLA_NOPE], jnp.zeros((MLA_KV_LORA, MLA_HEADS, HEAD_PAD - MLA_NOPE), F32)],
                          axis=2).reshape(MLA_KV_LORA, -1).astype(BF16)
    wvt = ukv[:, :, MLA_NOPE:].reshape(MLA_KV_LORA, -1).T.astype(BF16)
    wgvt = seg(5).T.astype(BF16)

    gq, gk = gqa_q_norm[i], gqa_k_norm[i]
    gqs, gks = _swap_halves(gq, 16), _swap_halves(gk, 16)
    two = lambda v: jnp.concatenate([v, v])[None, :]
    wg = jnp.zeros((128, 256), F32)
    wg = wg.at[:GLA_GATE_RANK, :128].set(gate_fw_w[i]).at[GLA_GATE_RANK:2 * GLA_GATE_RANK, 128:].set(gate_bw_w[i])
    return dict(
        nw=mix_norm[i][None, :], wbig=wbig, qn=mla_q_norm[i][None, :], kvn=mla_kv_norm[i][None, :],
        wqa=wqa, wqb=wqb, wka=wka, wvt=wvt, gqc=two(gq), gqs=two(gqs), gkc=two(gk), gks=two(gks),
        wgvt=wgvt, wg=wg.astype(BF16), bg=jnp.concatenate([gate_fw_b[i], gate_bw_b[i]])[None, :])


def _assemble(lay, parts):
    pieces = [parts["p"][0], parts["s"][0]]
    for name, nxt in (("p", lay["s"]["meta_off"]), ("s", lay["R"])):
        m = parts[name][1][:lay[name]["B"] * N_META]
        pad = nxt - lay[name]["meta_off"] - m.shape[0]
        pieces += [m, jnp.zeros((pad, m.shape[1]), m.dtype)]
    return jnp.concatenate(pieces, axis=0)


def kernel(x_prompt, x_sample, meta_tokens, final_norm, ffn1_norm, ffn1_w_in, ffn1_w_out, mix_norm, w_in, w_out,
           mla_q_norm, mla_w_uq, mla_kv_norm, mla_w_ukv, gqa_q_norm, gqa_k_norm, gla_gate_fw_w, gla_gate_fw_b,
           gla_gate_bw_w, gla_gate_bw_b, gla_out_norm, ffn2_norm, ffn2_w_in, ffn2_w_out):
    bp, sp, _ = x_prompt.shape
    bs, ss, _ = x_sample.shape
    depth = w_in.shape[0]
    lay = _layout(bp, sp, bs, ss)
    tile_tab, smax = _tile_table(lay)
    tabs = _rope_tables(smax)
    meta_rows = []
    for name, nxt in (("p", lay["s"]["meta_off"]), ("s", lay["R"])):
        g = lay[name]
        meta_rows += [jnp.tile(meta_tokens, (g["B"], 1)),
                      jnp.zeros((nxt - g["meta_off"] - g["B"] * N_META, D_MODEL), F32)]
    h = (x_prompt.reshape(-1, D_MODEL), x_sample.reshape(-1, D_MODEL), jnp.concatenate(meta_rows, axis=0))
    wnorm = jnp.tile(gla_out_norm, (1, GLA_HEADS))
    n_mla = MLA_HEADS * MLA_V
    tq = {"p": min(TM, sp), "s": min(TM, ss)}

    def ffn_w(w_i, w_o):
        return w_i[:, :D_FF].astype(BF16), w_i[:, D_FF:].astype(BF16), w_o.astype(BF16)

    y = None
    for i in range(depth):
        h = _ffn_call(h, ffn1_norm[i][None, :], *ffn_w(ffn1_w_in[i], ffn1_w_out[i]), lay)
        consts = _mixer_consts(i, mix_norm, w_in, mla_q_norm, mla_w_uq, mla_kv_norm, mla_w_ukv, gqa_q_norm,
                               gqa_k_norm, gla_gate_fw_w, gla_gate_fw_b, gla_gate_bw_w, gla_gate_bw_b)
        qm, km, vmt, qg, kg, vgt, lq, lk, lv, gfw, gbw, lgate = _mixin_call(h, tile_tab, consts, tabs)
        o_mla, o_gqa, o_gla = {}, {}, {}
        for name in ("p", "s"):
            g = lay[name]
            o_mla[name] = (_attn_call(qm, km, vmt, g, gqa=False, meta_q=False, tq=tq[name]),
                           _attn_call(qm, km, vmt, g, gqa=False, meta_q=True, tq=None))
            o_gqa[name] = (_attn_call(qg, kg, vgt, g, gqa=True, meta_q=False, tq=tq[name]),
                           _attn_call(qg, kg, vgt, g, gqa=True, meta_q=True, tq=None))
            o_gla[name] = _gla_call(lq, lk, lv, gfw, gbw, lgate, wnorm[i][None, :], g)
        wo = w_out[i].astype(BF16)
        proj = (_assemble(lay, o_mla), _assemble(lay, o_gqa), _assemble(lay, o_gla),
                wo[:n_mla], wo[n_mla:2 * n_mla], wo[2 * n_mla:])
        last = i == depth - 1
        out = _ffn_call(h, ffn2_norm[i][None, :], *ffn_w(ffn2_w_in[i], ffn2_w_out[i]), lay, proj=proj,
                        final=final_norm[None, :] if last else None)
        if last:
            y = out
        else:
            h = out
    return y[0].reshape(bp, sp, D_MODEL), y[1].reshape(bs, ss, D_MODEL)
```

```python
import functools

import numpy as np
import jax
import jax.numpy as jnp
from jax import lax
from jax.experimental import pallas as pl
from jax.experimental.pallas import tpu as pltpu

F32 = jnp.float32
BF16 = jnp.bfloat16

D_MODEL = 1024
N_META = 16
GRID_W = 64
NORM_EPS = 1e-6
ROPE_THETA = 10000.0
D_FF = 2816
MLA_HEADS = 6
MLA_Q_LORA = 256
MLA_KV_LORA = 128
MLA_NOPE = 64
MLA_ROPE = 32
MLA_V = 64
GQA_HEADS = 6
GQA_KV_HEADS = 2
GQA_HEAD_DIM = 64
GLA_HEADS = 4
GLA_DK = 32
GLA_DV = 64
GLA_GATE_RANK = 16
GLA_GATE_NORM = 16.0
GLA_CHUNK = 64

LANES = 128
TM = 512
META_BLOCK = 128
HEAD_PAD = 128
VMEM_LIMIT = 56 * 1024 * 1024
FF_SPLITS = (0, 1536, D_FF)
ATTN_TK = 512
ATTN_MAX_EXCESS = 64.0
ATTN_UNROLL = 16
ATTN_STATIC_CHUNKS = 4

LOG2E = 1.4426950408889634
MLA_SCALE = (MLA_NOPE + MLA_ROPE) ** -0.5 * LOG2E
GQA_SCALE = GQA_HEAD_DIM ** -0.5 * LOG2E
SUM_ROWS = 16
VROWS_MLA = SUM_ROWS + MLA_V
VROWS_GQA = GQA_KV_HEADS * (SUM_ROWS + GQA_HEAD_DIM)
GLA_QSCALE = GLA_DK ** -0.5

C_CQ = 0
C_CKV = C_CQ + MLA_Q_LORA
C_KP = C_CKV + MLA_KV_LORA
C_KPR = C_KP + HEAD_PAD
C_GQ = C_KPR + HEAD_PAD
C_GQR = C_GQ + GQA_HEADS * HEAD_PAD
C_GK = C_GQR + GQA_HEADS * HEAD_PAD
C_GKR = C_GK + 128
C_LQ = C_GKR + 128
C_LK = C_LQ + 128
C_LV = C_LK + 128
C_LGATE = C_LV + 256
C_LG = C_LGATE + 256
C_END = C_LG + 128

NT_DIMS = (((1,), (1,)), ((), ()))
TN_DIMS = (((0,), (0,)), ((), ()))


def _dot(a, b):
    return jnp.dot(a, b, preferred_element_type=F32)


def _dot_nt(a, b):
    return lax.dot_general(a, b, NT_DIMS, preferred_element_type=F32)


def _dot_tn(a, b):
    return lax.dot_general(a, b, TN_DIMS, preferred_element_type=F32)


def _rms(x, w):
    return x * lax.rsqrt(jnp.mean(x * x, axis=-1, keepdims=True) + NORM_EPS) * w


def _params(sem):
    return pltpu.CompilerParams(dimension_semantics=sem, vmem_limit_bytes=VMEM_LIMIT)


def _const_spec(shape):
    nd = len(shape)
    return pl.BlockSpec(shape, lambda *_: (0,) * nd, pipeline_mode=pl.Buffered(1))


def _pick(i, refs, bounds):
    v = refs[-1][...]
    for k in reversed(range(len(refs) - 1)):
        v = jnp.where(i < bounds[k + 1], refs[k][...], v)
    return v


def _ffn_body(*refs, src_bounds, proj_bounds, has_final, tiles_p, tiles_s):
    it = iter(refs)
    take = lambda n: [next(it) for _ in range(n)]
    i = pl.program_id(0)
    h = _pick(i, take(len(src_bounds) - 1), src_bounds)
    if proj_bounds is not None:
        o_parts = [take(len(proj_bounds) - 1) for _ in range(3)]
        w_parts = take(3)
    nw_ref, wg_ref, wu_ref, wo_ref = take(4)
    if has_final:
        fw_ref, yp_ref, ys_ref = take(3)
    else:
        (out_ref,) = take(1)

    if proj_bounds is not None:
        for o_refs, w_ref in zip(o_parts, w_parts):
            h = h + _dot(_pick(i, o_refs, proj_bounds), w_ref[...])
    xn = _rms(h, nw_ref[...]).astype(BF16)
    acc = jnp.zeros(h.shape, F32)
    for lo, hi in zip(FF_SPLITS[:-1], FF_SPLITS[1:]):
        g = _dot(xn, wg_ref[:, lo:hi])
        u = _dot(xn, wu_ref[:, lo:hi])
        a = (g * jax.nn.sigmoid(g) * u).astype(BF16)
        acc = acc + _dot(a, wo_ref[lo:hi, :])
    h2 = h + 0.5 * acc
    if has_final:
        y = _rms(h2, fw_ref[...])

        @pl.when(i < tiles_p)
        def _():
            yp_ref[...] = y

        @pl.when(jnp.logical_and(i >= tiles_p, i < tiles_p + tiles_s))
        def _():
            ys_ref[...] = y
    else:
        out_ref[...] = h2


def _pieces(x):
    arrs = list(x) if isinstance(x, (tuple, list)) else [x]
    counts = [a.shape[0] // TM for a in arrs]
    bounds = [0]
    for c in counts:
        bounds.append(bounds[-1] + c)
    specs = [pl.BlockSpec((TM, a.shape[1]), lambda i, f=f, c=c: (jnp.clip(i - f, 0, c - 1), 0))
             for a, f, c in zip(arrs, bounds, counts)]
    return arrs, specs, tuple(bounds)


def _ffn_call(h, nw, wg, wu, wo, lay, proj=None, final=None):
    R = lay["R"]
    n_tiles = R // TM
    tiles_p, tiles_s = lay["p"]["main_rows"] // TM, lay["s"]["main_rows"] // TM
    ins, specs, src_bounds = _pieces(h)
    assert src_bounds[-1] == n_tiles
    proj_bounds = None
    if proj is not None:
        oa, ob, oc, wa, wb, wc = proj
        for o in (oa, ob, oc):
            arrs, sp, proj_bounds = _pieces(o)
            assert proj_bounds[-1] == n_tiles
            ins += arrs
            specs += sp
        ins += [wa, wb, wc]
        specs += [_const_spec(wa.shape), _const_spec(wb.shape), _const_spec(wc.shape)]
    ins += [nw, wg, wu, wo]
    specs += [_const_spec(nw.shape), _const_spec(wg.shape), _const_spec(wu.shape), _const_spec(wo.shape)]
    if final is not None:
        ins.append(final)
        specs.append(_const_spec(final.shape))
        out_shape = (jax.ShapeDtypeStruct((tiles_p * TM, D_MODEL), F32),
                     jax.ShapeDtypeStruct((tiles_s * TM, D_MODEL), F32))
        part = lambda first, count: pl.BlockSpec((TM, D_MODEL), lambda i: (jnp.clip(i - first, 0, count - 1), 0))
        out_specs = (part(0, tiles_p), part(tiles_p, tiles_s))
    else:
        out_shape = jax.ShapeDtypeStruct((R, D_MODEL), F32)
        out_specs = pl.BlockSpec((TM, D_MODEL), lambda i: (i, 0))
    body = functools.partial(_ffn_body, src_bounds=src_bounds, proj_bounds=proj_bounds, has_final=final is not None,
                             tiles_p=tiles_p, tiles_s=tiles_s)
    return pl.pallas_call(
        body, grid=(n_tiles,), in_specs=specs, out_specs=out_specs, out_shape=out_shape,
        compiler_params=_params(("arbitrary",)),
        name="ffn" + ("_proj" if proj is not None else "") + ("_final" if final is not None else ""),
    )(*ins)


def _mixin_body(tab_ref, h_ref, nw_ref, wbig_ref, cm_ref, sm_ref, cg_ref, sg_ref,
                qn_ref, kvn_ref, wqa_ref, wqb_ref, wka_ref, wvt_ref,
                gqc_ref, gqs_ref, gkc_ref, gks_ref, wgvt_ref, wg_ref, bg_ref,
                qm_ref, km_ref, vmt_ref, qg_ref, kg_ref, vgt_ref,
                lq_ref, lk_ref, lv_ref, gfw_ref, gbw_ref, lgate_ref):
    del tab_ref
    hn = _rms(h_ref[...], nw_ref[...]).astype(BF16)

    pall = _dot(hn, wbig_ref[...])

    def proj(a, width):
        return pall[:, a:a + width]

    cm, sm = cm_ref[...], sm_ref[...]
    cg, sg = cg_ref[...], sg_ref[...]

    cqn = _rms(proj(C_CQ, MLA_Q_LORA), qn_ref[...]).astype(BF16)
    qa = _dot(cqn, wqa_ref[...])
    qb = _dot(cqn, wqb_ref[...])
    for hd in range(MLA_HEADS):
        blk = slice(hd * HEAD_PAD, (hd + 1) * HEAD_PAD)
        qm_ref[:, blk] = ((qa[:, blk] * cm + qb[:, blk] * sm) * MLA_SCALE).astype(BF16)

    kvn = _rms(proj(C_CKV, MLA_KV_LORA), kvn_ref[...]).astype(BF16)
    kpe = proj(C_KP, HEAD_PAD) * cm + proj(C_KPR, HEAD_PAD) * sm
    ka = _dot(kvn, wka_ref[...])
    for hd in range(MLA_HEADS):
        blk = slice(hd * HEAD_PAD, (hd + 1) * HEAD_PAD)
        km_ref[:, blk] = (ka[:, blk] + kpe).astype(BF16)
    ones = jnp.ones((SUM_ROWS, hn.shape[0]), F32)
    vt = _dot_nt(wvt_ref[...], kvn)
    pieces = []
    for hd in range(MLA_HEADS):
        pieces += [ones, vt[hd * MLA_V:(hd + 1) * MLA_V]]
    vmt_ref[0] = jnp.concatenate(pieces, axis=0).astype(BF16)

    tqc, tqs = gqc_ref[...] * cg, gqs_ref[...] * sg
    for hd in range(GQA_HEADS):
        x = proj(C_GQ + hd * HEAD_PAD, HEAD_PAD)
        xr = proj(C_GQR + hd * HEAD_PAD, HEAD_PAD)
        n = lax.rsqrt(jnp.sum(x * x, axis=-1, keepdims=True) * (1.0 / GQA_HEAD_DIM) + NORM_EPS)
        qg_ref[:, hd * HEAD_PAD:(hd + 1) * HEAD_PAD] = (n * (x * tqc + xr * tqs) * GQA_SCALE).astype(BF16)
    x = proj(C_GK, 128)
    xr = proj(C_GKR, 128)
    lo = lax.broadcasted_iota(jnp.int32, x.shape, 1) < GQA_HEAD_DIM
    x2 = x * x
    ms0 = jnp.sum(jnp.where(lo, x2, 0.0), axis=-1, keepdims=True) * (1.0 / GQA_HEAD_DIM)
    ms1 = jnp.sum(jnp.where(lo, 0.0, x2), axis=-1, keepdims=True) * (1.0 / GQA_HEAD_DIM)
    n = jnp.where(lo, lax.rsqrt(ms0 + NORM_EPS), lax.rsqrt(ms1 + NORM_EPS))
    kg_ref[...] = (n * (x * (gkc_ref[...] * cg) + xr * (gks_ref[...] * sg))).astype(BF16)
    vg = _dot_nt(wgvt_ref[...], hn)
    vgt_ref[0] = jnp.concatenate([ones, vg[:GQA_HEAD_DIM], ones, vg[GQA_HEAD_DIM:]], axis=0).astype(BF16)

    lq_ref[...] = proj(C_LQ, 128) * GLA_QSCALE
    lk_ref[...] = proj(C_LK, 128)
    lv_ref[...] = proj(C_LV, 256)
    lgate_ref[...] = proj(C_LGATE, 256)
    gg = _dot(proj(C_LG, 128).astype(BF16), wg_ref[...]) + bg_ref[...]
    ls = (jnp.minimum(gg, 0.0) - jnp.log1p(jnp.exp(-jnp.abs(gg)))) * (1.0 / GLA_GATE_NORM)
    gfw_ref[...] = ls[:, :128]
    gbw_ref[...] = ls[:, 128:]


def _mixin_call(h, tile_tab, consts, tabs):
    R = h.shape[0]
    n_tiles = R // TM
    row = lambda w: pl.BlockSpec((TM, w), lambda i, t: (i, 0))
    tabspec = pl.BlockSpec((TM, LANES), lambda i, t: (t[i], 0))
    cs = lambda a: _const_spec(a.shape)
    col3 = lambda r: pl.BlockSpec((1, r, TM), lambda i, t: (i, 0, 0))
    ins = [h, consts["nw"], consts["wbig"], tabs["cm"], tabs["sm"], tabs["cg"], tabs["sg"],
           consts["qn"], consts["kvn"], consts["wqa"], consts["wqb"], consts["wka"], consts["wvt"],
           consts["gqc"], consts["gqs"], consts["gkc"], consts["gks"], consts["wgvt"],
           consts["wg"], consts["bg"]]
    specs = [row(D_MODEL), cs(ins[1]), cs(ins[2]), tabspec, tabspec, tabspec, tabspec] + [cs(a) for a in ins[7:]]
    W6 = MLA_HEADS * HEAD_PAD
    out_shape = (
        jax.ShapeDtypeStruct((R, W6), BF16), jax.ShapeDtypeStruct((R, W6), BF16),
        jax.ShapeDtypeStruct((n_tiles, MLA_HEADS * VROWS_MLA, TM), BF16),
        jax.ShapeDtypeStruct((R, W6), BF16), jax.ShapeDtypeStruct((R, 128), BF16),
        jax.ShapeDtypeStruct((n_tiles, VROWS_GQA, TM), BF16),
        jax.ShapeDtypeStruct((R, 128), F32), jax.ShapeDtypeStruct((R, 128), F32),
        jax.ShapeDtypeStruct((R, 256), F32), jax.ShapeDtypeStruct((R, 128), F32),
        jax.ShapeDtypeStruct((R, 128), F32), jax.ShapeDtypeStruct((R, 256), F32),
    )
    out_specs = (row(W6), row(W6), col3(MLA_HEADS * VROWS_MLA), row(W6), row(128), col3(VROWS_GQA),
                 row(128), row(128), row(256), row(128), row(128), row(256))
    gs = pltpu.PrefetchScalarGridSpec(num_scalar_prefetch=1, grid=(n_tiles,), in_specs=specs, out_specs=out_specs)
    return pl.pallas_call(_mixin_body, grid_spec=gs, out_shape=out_shape,
                          compiler_params=_params(("arbitrary",)), name="mixer_in")(tile_tab, *ins)


def _attn_body(q_ref, k_ref, kmeta_ref, vt_ref, vtmeta_ref, o_ref, *scratch,
               gqa, tq, n_chunks, meta_q, b_axis, p_axis, unroll):
    b = pl.program_id(b_axis)
    p = pl.program_id(p_axis)
    slot = b % (META_BLOCK // N_META)
    rows = lax.broadcasted_iota(jnp.int32, (META_BLOCK, 1), 0)
    meta_valid = (rows // N_META) == slot
    zeros = jnp.zeros((VROWS_MLA, tq), F32)

    def exact_update(carry, s_t, v_t):
        m, acc = carry
        m_new = jnp.maximum(m, jnp.max(s_t, axis=0, keepdims=True))
        p_t = jnp.exp2(s_t - m_new).astype(BF16)
        return m_new, jnp.exp2(m - m_new) * acc + _dot(v_t, p_t)

    def fast_update(carry, s_t, v_t):
        m, acc, excess = carry
        p_t = jnp.exp2(s_t - m).astype(BF16)
        cmax = jnp.max(s_t, axis=0, keepdims=True)
        m_new = jnp.maximum(m, cmax)
        return m_new, jnp.exp2(m - m_new) * (acc + _dot(v_t, p_t)), jnp.maximum(excess, cmax - m)

    def heads(fast):
        update = fast_update if fast else exact_update
        parts = TM // ATTN_TK
        q = [q_ref[:, i * HEAD_PAD:(i + 1) * HEAD_PAD] for i in range(2)]
        ks = [slice(0, HEAD_PAD) if gqa else slice(i * HEAD_PAD, (i + 1) * HEAD_PAD) for i in range(2)]
        if gqa:
            groups = [(2 * p + i) // (GQA_HEADS // GQA_KV_HEADS) for i in range(2)]
            vs = [pl.ds(pl.multiple_of(g * VROWS_MLA, SUM_ROWS), VROWS_MLA) for g in groups]
        else:
            vs = [pl.ds(i * VROWS_MLA, VROWS_MLA) for i in range(2)]

        def scores(i, c):
            return _dot_nt(k_ref[pl.ds(pl.multiple_of(c * ATTN_TK, ATTN_TK), ATTN_TK), ks[i]], q[i])

        def values(i, tile, part):
            return vt_ref[tile, vs[i], part * ATTN_TK:(part + 1) * ATTN_TK]

        def meta_scores(i):
            return jnp.where(meta_valid, _dot_nt(kmeta_ref[:, ks[i]], q[i]), -jnp.inf)

        def init(s0):
            if fast:
                return jnp.max(s0, axis=0, keepdims=True), zeros, jnp.zeros((1, tq), F32)
            return jnp.full((1, tq), -jnp.inf, F32), zeros

        if n_chunks <= ATTN_STATIC_CHUNKS:
            out = []
            for i in range(2):
                s_cur = scores(i, 0)
                carry = init(s_cur)
                for c in range(n_chunks):
                    s_next = scores(i, c + 1) if c + 1 < n_chunks else meta_scores(i)
                    carry = update(carry, s_cur, values(i, c // parts, c % parts))
                    s_cur = s_next
                out.append(update(carry, s_cur, vtmeta_ref[0, vs[i], :]))
            return out

        s_buf = scratch[-1]
        carry = []
        for i in range(2):
            s0 = scores(i, 0)
            s_buf[i, 0] = s0
            carry.append(init(s0))

        def group_step(j, st):
            st = list(st)
            for u in range(unroll):
                c = j * unroll + u
                for i in range(2):
                    s_buf[i, (u + 1) % 2] = scores(i, jnp.minimum(c + 1, n_chunks - 1))
                    st[i] = update(st[i], s_buf[i, u % 2], values(i, j * (unroll // parts) + u // parts, u % parts))
            return tuple(st)

        carry = lax.fori_loop(0, n_chunks // unroll, group_step, tuple(carry))
        return [update(carry[i], meta_scores(i), vtmeta_ref[0, vs[i], :]) for i in range(2)]

    if n_chunks <= ATTN_STATIC_CHUNKS:
        fast = heads(True)
        worst = jnp.max(jnp.maximum(fast[0][2], fast[1][2]))
        accs = lax.cond(worst > ATTN_MAX_EXCESS,
                        lambda: tuple(c[1] for c in heads(False)),
                        lambda: tuple(c[1] for c in fast))
    else:
        accs = [c[1] for c in heads(False)]
    halves = [acc[SUM_ROWS:] / acc[0:1] for acc in accs]
    o = jnp.concatenate(halves, axis=0).T.astype(BF16)
    if meta_q:
        stage = scratch[0]

        @pl.when(slot == 0)
        def _():
            o_ref[...] = jnp.zeros(o_ref.shape, o_ref.dtype)

        stage[...] = o
        off = pl.multiple_of(slot * N_META, N_META)
        o_ref[pl.ds(off, N_META), :] = stage[pl.ds(off, N_META), :]
    else:
        o_ref[...] = o


def _attn_call(q, k, vt, g, *, gqa, meta_q, tq):
    B, S = g["B"], g["S"]
    n_chunks = S // ATTN_TK
    n_tiles = S // TM
    kw = HEAD_PAD if gqa else 2 * HEAD_PAD
    vrows = VROWS_GQA if gqa else 2 * VROWS_MLA
    main_blk = g["main_off"] // S
    meta_blk = g["meta_off"] // META_BLOCK
    per_blk = META_BLOCK // N_META
    pk = (lambda p: 0) if gqa else (lambda p: p)
    if meta_q:
        grid = (3, B)
        ax = lambda f: (lambda p, b: f(b, p, 0))
        tq = META_BLOCK
        q_map = lambda b, p, t: (meta_blk + b // per_blk, p)
        o_map = lambda b, p, t: (b // per_blk, p)
        out_rows = pl.cdiv(B * N_META, META_BLOCK) * META_BLOCK
        sem = ("arbitrary", "arbitrary")
        scratch = [pltpu.VMEM((META_BLOCK, HEAD_PAD), BF16)]
        b_axis, p_axis = 1, 0
    else:
        nq = S // tq
        grid = (B, 3, nq)
        ax = lambda f: f
        q_map = lambda b, p, t: ((g["main_off"] + b * S) // tq + t, p)
        o_map = lambda b, p, t: (b * nq + t, p)
        out_rows = B * S
        sem = ("arbitrary", "arbitrary", "arbitrary")
        scratch = []
        b_axis, p_axis = 0, 1
    in_specs = [
        pl.BlockSpec((tq, 2 * HEAD_PAD), ax(q_map)),
        pl.BlockSpec((S, kw), ax(lambda b, p, t: (main_blk + b, pk(p)))),
        pl.BlockSpec((META_BLOCK, kw), ax(lambda b, p, t: (meta_blk + b // per_blk, pk(p)))),
        pl.BlockSpec((n_tiles, vrows, TM), ax(lambda b, p, t: (main_blk + b, pk(p), 0))),
        pl.BlockSpec((1, vrows, META_BLOCK),
                     ax(lambda b, p, t: (g["meta_off"] // TM + (b * N_META) // TM, pk(p),
                                         ((b * N_META) % TM) // META_BLOCK))),
    ]
    unroll = None
    if n_chunks > ATTN_STATIC_CHUNKS:
        step = 2 * TM // ATTN_TK
        unroll = max(u for u in range(step, ATTN_UNROLL + 1, step) if n_chunks % u == 0)
        scratch = scratch + [pltpu.VMEM((2, 2, ATTN_TK, tq), F32)]
    body = functools.partial(_attn_body, gqa=gqa, tq=tq, n_chunks=n_chunks, meta_q=meta_q,
                             b_axis=b_axis, p_axis=p_axis, unroll=unroll)
    return pl.pallas_call(
        body, grid=grid, in_specs=in_specs,
        out_specs=pl.BlockSpec((tq, HEAD_PAD), ax(o_map)),
        out_shape=jax.ShapeDtypeStruct((out_rows, 3 * HEAD_PAD), BF16),
        scratch_shapes=scratch, compiler_params=_params(sem),
        name=("gqa" if gqa else "mla") + ("_metaq_" if meta_q else "_") + g["name"],
    )(q, k, k, vt, vt)


def _gla_consts():
    c = GLA_CHUNK
    r = lax.broadcasted_iota(jnp.int32, (c, c), 0)
    s = lax.broadcasted_iota(jnp.int32, (c, c), 1)
    r4 = lax.broadcasted_iota(jnp.int32, (GLA_HEADS * c, c), 0) % c
    s4 = lax.broadcasted_iota(jnp.int32, (GLA_HEADS * c, c), 1)
    kl = lax.broadcasted_iota(jnp.int32, (1, GLA_HEADS * GLA_DK), 1) // GLA_DK
    vl = lax.broadcasted_iota(jnp.int32, (1, GLA_HEADS * GLA_DV), 1) // GLA_DV
    kr = lax.broadcasted_iota(jnp.int32, (GLA_HEADS * GLA_DK, GLA_HEADS * GLA_DV), 0) // GLA_DK
    vc = lax.broadcasted_iota(jnp.int32, (GLA_HEADS * GLA_DK, GLA_HEADS * GLA_DV), 1) // GLA_DV
    return dict(
        tril=(s <= r).astype(BF16), triu=(s >= r).astype(BF16),
        mask4_fw=s4 <= r4, mask4_bw=s4 >= r4,
        kmask=[(kl == h).astype(F32) for h in range(GLA_HEADS)],
        vmask=[(vl == h).astype(F32) for h in range(GLA_HEADS)],
        blockdiag=(kr == vc).astype(F32),
        ones=jnp.ones((c, GLA_HEADS * GLA_DV), BF16),
    )


def _gla_chunks(qs, ks, vs, gs, state, cst, fwd):
    c, n = GLA_CHUNK, len(qs)
    nk = GLA_HEADS * GLA_DK
    tri = cst["tril"] if fwd else cst["triu"]
    hilo = []
    for g in gs:
        hi = g.astype(BF16)
        hilo.append(jnp.concatenate([hi, (g - hi.astype(F32)).astype(BF16)], axis=1))
    bcum = [_dot(tri, x) for x in hilo]
    bcum = [x[:, :nk] + x[:, nk:] for x in bcum]
    ltot = [_dot_tn(x, cst["ones"]) for x in hilo]
    decay = [jnp.exp(x[:nk] + x[nk:]) for x in ltot]
    qe = [q * jnp.exp(b) for q, b in zip(qs, bcum)]
    ke = [(k * jnp.exp(-b)).astype(BF16) for k, b in zip(ks, bcum)]
    kd = [(k * jnp.exp((b[c - 1:c] if fwd else b[0:1]) - b)).astype(BF16) for k, b in zip(ks, bcum)]
    vb = [v.astype(BF16) for v in vs]
    qstack = [jnp.concatenate([x * cst["kmask"][h] for h in range(GLA_HEADS)], axis=0).astype(BF16) for x in qe]
    mask = cst["mask4_fw"] if fwd else cst["mask4_bw"]
    att = [jnp.where(mask, _dot_nt(x, y), 0.0).astype(BF16) for x, y in zip(qstack, ke)]
    full = [_dot(x, y) for x, y in zip(att, vb)]
    dstate = [_dot_tn(x, y) * cst["blockdiag"] for x, y in zip(kd, vb)]
    states = []
    for i in range(n):
        states.append(state.astype(BF16))
        state = decay[i] * state + dstate[i]
    outs = []
    for i in range(n):
        o = _dot(qe[i].astype(BF16), states[i])
        for h in range(GLA_HEADS):
            o = o + full[i][h * c:(h + 1) * c] * cst["vmask"][h]
        outs.append(o)
    return outs, state


def _meta_chunk(ref):
    x = ref[...]
    return jnp.concatenate([jnp.zeros((GLA_CHUNK - N_META, x.shape[1]), x.dtype), x], axis=0)


def _gla_fw_body(q_ref, k_ref, v_ref, g_ref, qm_ref, km_ref, vm_ref, gm_ref, o_ref, om_ref, state):
    cst = _gla_consts()

    @pl.when(pl.program_id(1) == 0)
    def _():
        (o,), s = _gla_chunks([_meta_chunk(qm_ref)], [_meta_chunk(km_ref)], [_meta_chunk(vm_ref)],
                              [_meta_chunk(gm_ref)], jnp.zeros(state.shape, F32), cst, True)
        om_ref[...] = o[GLA_CHUNK - N_META:]
        state[...] = s

    sls = [slice(c * GLA_CHUNK, (c + 1) * GLA_CHUNK) for c in range(TM // GLA_CHUNK)]
    outs, s = _gla_chunks([q_ref[sl, :] for sl in sls], [k_ref[sl, :] for sl in sls], [v_ref[sl, :] for sl in sls],
                          [g_ref[sl, :] for sl in sls], state[...], cst, True)
    for sl, o in zip(sls, outs):
        o_ref[sl, :] = o
    state[...] = s


def _gla_finish(o, gate, w, bd):
    x2 = o * o
    hi = x2.astype(BF16)
    lo = (x2 - hi.astype(F32)).astype(BF16)
    ms = _dot(hi, bd) + _dot(lo, bd)
    return (o * lax.rsqrt(ms + NORM_EPS) * w * (gate * jax.nn.sigmoid(gate))).astype(BF16)


def _gla_bw_body(q_ref, k_ref, v_ref, g_ref, ofw_ref, gate_ref, qm_ref, km_ref, vm_ref, gm_ref, ofwm_ref, gatem_ref,
                 w_ref, o_ref, om_ref, state):
    cst = _gla_consts()
    t = pl.program_id(1)
    r = lax.broadcasted_iota(jnp.int32, (GLA_HEADS * GLA_DV,) * 2, 0) // GLA_DV
    c_ = lax.broadcasted_iota(jnp.int32, (GLA_HEADS * GLA_DV,) * 2, 1) // GLA_DV
    bd = jnp.where(r == c_, 1.0 / GLA_DV, 0.0).astype(BF16)

    @pl.when(t == 0)
    def _():
        state[...] = jnp.zeros(state.shape, F32)

    sls = [slice(c * GLA_CHUNK, (c + 1) * GLA_CHUNK) for c in reversed(range(TM // GLA_CHUNK))]
    outs, s = _gla_chunks([q_ref[sl, :] for sl in sls], [k_ref[sl, :] for sl in sls], [v_ref[sl, :] for sl in sls],
                          [g_ref[sl, :] for sl in sls], state[...], cst, False)
    for sl, o in zip(sls, outs):
        o_ref[sl, :] = _gla_finish(o + ofw_ref[sl, :], gate_ref[sl, :], w_ref[...], bd)
    state[...] = s

    @pl.when(t == pl.num_programs(1) - 1)
    def _():
        (o,), _ = _gla_chunks([_meta_chunk(qm_ref)], [_meta_chunk(km_ref)], [_meta_chunk(vm_ref)],
                              [_meta_chunk(gm_ref)], s, cst, False)
        om_ref[...] = _gla_finish(o[GLA_CHUNK - N_META:] + ofwm_ref[...], gatem_ref[...], w_ref[...], bd)


def _gla_call(lq, lk, lv, gfw, gbw, lgate, wnorm, g):
    B, S = g["B"], g["S"]
    nt = S // TM
    main0 = g["main_off"] // TM
    meta0 = g["meta_off"] // N_META
    mrow = lambda w, rev: pl.BlockSpec(
        (TM, w), (lambda b, t: (main0 + b * nt + (nt - 1 - t), 0)) if rev else (lambda b, t: (main0 + b * nt + t, 0)))
    meta = lambda w: pl.BlockSpec((N_META, w), lambda b, t: (meta0 + b, 0))
    orow = lambda w, rev: pl.BlockSpec(
        (TM, w), (lambda b, t: (b * nt + (nt - 1 - t), 0)) if rev else (lambda b, t: (b * nt + t, 0)))
    ometa = lambda w: pl.BlockSpec((N_META, w), lambda b, t: (b, 0))
    sem = _params(("arbitrary", "arbitrary"))
    state = [pltpu.VMEM((GLA_HEADS * GLA_DK, GLA_HEADS * GLA_DV), F32)]
    ofw, ofw_meta = pl.pallas_call(
        _gla_fw_body, grid=(B, nt),
        in_specs=[mrow(128, False), mrow(128, False), mrow(256, False), mrow(128, False),
                  meta(128), meta(128), meta(256), meta(128)],
        out_specs=(orow(256, False), ometa(256)),
        out_shape=(jax.ShapeDtypeStruct((B * S, 256), F32), jax.ShapeDtypeStruct((B * N_META, 256), F32)),
        scratch_shapes=state, compiler_params=sem, name="gla_fw_" + g["name"],
    )(lq, lk, lv, gfw, lq, lk, lv, gfw)
    o, o_meta = pl.pallas_call(
        _gla_bw_body, grid=(B, nt),
        in_specs=[mrow(128, True), mrow(128, True), mrow(256, True), mrow(128, True), orow(256, True), mrow(256, True),
                  meta(128), meta(128), meta(256), meta(128), ometa(256), meta(256), _const_spec(wnorm.shape)],
        out_specs=(orow(256, True), ometa(256)),
        out_shape=(jax.ShapeDtypeStruct((B * S, 256), BF16), jax.ShapeDtypeStruct((B * N_META, 256), BF16)),
        scratch_shapes=state, compiler_params=sem, name="gla_bw_" + g["name"],
    )(lq, lk, lv, gbw, ofw, lgate, lq, lk, lv, gbw, ofw_meta, lgate, wnorm)
    return o, o_meta


def _rot_cols(w, n):
    s = w.shape
    w4 = w.reshape(s[:-1] + (s[-1] // (2 * n), 2, n))
    return jnp.stack([-w4[..., 1, :], w4[..., 0, :]], axis=-2).reshape(s)


def _swap_halves(w, n):
    w3 = w.reshape(-1, 2, n)
    return w3[:, ::-1, :].reshape(w.shape)


def _layout(bp, sp, bs, ss):
    up = lambda x: -(-x // TM) * TM
    lay = {"p": dict(name="p", B=bp, S=sp, main_off=0, main_rows=bp * sp),
           "s": dict(name="s", B=bs, S=ss, main_off=bp * sp, main_rows=bs * ss)}
    lay["p"]["meta_off"] = bp * sp + bs * ss
    lay["s"]["meta_off"] = lay["p"]["meta_off"] + up(bp * N_META)
    lay["R"] = lay["s"]["meta_off"] + up(bs * N_META)
    for g in (lay["p"], lay["s"]):
        assert g["S"] % TM == 0 and g["main_off"] % g["S"] == 0 and g["S"] % GRID_W == 0
    return lay


def _tile_table(lay):
    smax = max(lay["p"]["S"], lay["s"]["S"])
    tab = []
    for g in (lay["p"], lay["s"]):
        tab += [t % (g["S"] // TM) for t in range(g["main_rows"] // TM)]
    tab += [smax // TM] * ((lay["R"] - lay["p"]["meta_off"]) // TM)
    return jnp.asarray(np.asarray(tab, np.int32)), smax


def _rope_tables(smax):
    r = jnp.arange(smax, dtype=jnp.int32)
    meta = jnp.arange(TM, dtype=jnp.int32) % N_META
    zero = jnp.zeros((TM,), jnp.int32)
    pos = jnp.concatenate([r + N_META, meta]).astype(F32)
    row = jnp.concatenate([r // GRID_W, zero]).astype(F32)
    col = jnp.concatenate([r % GRID_W, zero]).astype(F32)
    inv_m = ROPE_THETA ** (-jnp.arange(0, MLA_ROPE, 2, dtype=F32) / MLA_ROPE)
    half = GQA_HEAD_DIM // 2
    inv_g = ROPE_THETA ** (-jnp.arange(0, half, 2, dtype=F32) / half)
    a1, ar, ac = pos[:, None] * inv_m, row[:, None] * inv_g, col[:, None] * inv_g
    n = pos.shape[0]
    one, zer = jnp.ones((n, MLA_NOPE), F32), jnp.zeros((n, MLA_NOPE), F32)
    pad = jnp.zeros((n, HEAD_PAD - MLA_NOPE - MLA_ROPE), F32)
    cm = jnp.concatenate([one, jnp.cos(a1), jnp.cos(a1), pad], axis=1)
    sm = jnp.concatenate([zer, jnp.sin(a1), jnp.sin(a1), pad], axis=1)
    cg = jnp.concatenate([jnp.cos(ar), jnp.cos(ar), jnp.cos(ac), jnp.cos(ac)] * 2, axis=1)
    sg = jnp.concatenate([jnp.sin(ar), jnp.sin(ar), jnp.sin(ac), jnp.sin(ac)] * 2, axis=1)
    return dict(cm=cm, sm=sm, cg=cg, sg=sg)


def _mixer_consts(i, mix_norm, w_in, mla_q_norm, mla_w_uq, mla_kv_norm, mla_w_ukv, gqa_q_norm, gqa_k_norm,
                  gate_fw_w, gate_fw_b, gate_bw_w, gate_bw_b):
    w = w_in[i]
    z = lambda n: jnp.zeros((D_MODEL, n), F32)
    o = np.cumsum([0, MLA_Q_LORA, MLA_KV_LORA, MLA_ROPE, 384, 128, 128, 128, 128, 256, 16, 16, 256])
    seg = lambda j: w[:, o[j]:o[j + 1]]
    w_kr, w_gq, w_gk = seg(2), seg(3), seg(4)
    w_gqr = _rot_cols(w_gq, 16)
    tail = HEAD_PAD - MLA_NOPE - MLA_ROPE

    def gq_blocks(m):
        out = []
        for h in range(GQA_HEADS):
            blk = m[:, h * GQA_HEAD_DIM:(h + 1) * GQA_HEAD_DIM]
            out += [blk, z(GQA_HEAD_DIM)] if h < GQA_HEADS // GQA_KV_HEADS else [z(GQA_HEAD_DIM), blk]
        return out

    wbig = jnp.concatenate(
        [seg(0), seg(1), z(MLA_NOPE), w_kr, z(tail), z(MLA_NOPE), _rot_cols(w_kr, 16), z(tail)]
        + gq_blocks(w_gq) + gq_blocks(w_gqr)
        + [w_gk, _rot_cols(w_gk, 16), seg(6), seg(7), seg(8), seg(11), seg(9), seg(10), z(128 - 2 * GLA_GATE_RANK)],
        axis=1).astype(BF16)
    assert wbig.shape[1] == C_END

    uq = mla_w_uq[i].reshape(MLA_Q_LORA, MLA_HEADS, MLA_NOPE + MLA_ROPE)
    zq = jnp.zeros((MLA_Q_LORA, MLA_HEADS, tail), F32)
    wqa = jnp.concatenate([uq, zq], axis=2).reshape(MLA_Q_LORA, -1).astype(BF16)
    wqb = jnp.concatenate([jnp.zeros((MLA_Q_LORA, MLA_HEADS, MLA_NOPE), F32),
                           _rot_cols(uq[:, :, MLA_NOPE:], 16), zq], axis=2).reshape(MLA_Q_LORA, -1).astype(BF16)
    ukv = mla_w_ukv[i].reshape(MLA_KV_LORA, MLA_HEADS, MLA_NOPE + MLA_V)
    wka = jnp.concatenate([ukv[:, :, :MLA_NOPE], jnp.zeros((MLA_KV_LORA, MLA_HEADS, HEAD_PAD - MLA_NOPE), F32)],
                          axis=2).reshape(MLA_KV_LORA, -1).astype(BF16)
    wvt = ukv[:, :, MLA_NOPE:].reshape(MLA_KV_LORA, -1).T.astype(BF16)
    wgvt = seg(5).T.astype(BF16)

    gq, gk = gqa_q_norm[i], gqa_k_norm[i]
    gqs, gks = _swap_halves(gq, 16), _swap_halves(gk, 16)
    two = lambda v: jnp.concatenate([v, v])[None, :]
    wg = jnp.zeros((128, 256), F32)
    wg = wg.at[:GLA_GATE_RANK, :128].set(gate_fw_w[i]).at[GLA_GATE_RANK:2 * GLA_GATE_RANK, 128:].set(gate_bw_w[i])
    return dict(
        nw=mix_norm[i][None, :], wbig=wbig, qn=mla_q_norm[i][None, :], kvn=mla_kv_norm[i][None, :],
        wqa=wqa, wqb=wqb, wka=wka, wvt=wvt, gqc=two(gq), gqs=two(gqs), gkc=two(gk), gks=two(gks),
        wgvt=wgvt, wg=wg.astype(BF16), bg=jnp.concatenate([gate_fw_b[i], gate_bw_b[i]])[None, :])


def _row_pieces(lay, parts):
    pieces = [parts["p"][0], parts["s"][0]]
    for name, nxt in (("p", lay["s"]["meta_off"]), ("s", lay["R"])):
        m = parts[name][1][:lay[name]["B"] * N_META]
        pad = nxt - lay[name]["meta_off"] - m.shape[0]
        pieces.append(jnp.concatenate([m, jnp.zeros((pad, m.shape[1]), m.dtype)], axis=0))
    return pieces


def kernel(x_prompt, x_sample, meta_tokens, final_norm, ffn1_norm, ffn1_w_in, ffn1_w_out, mix_norm, w_in, w_out,
           mla_q_norm, mla_w_uq, mla_kv_norm, mla_w_ukv, gqa_q_norm, gqa_k_norm, gla_gate_fw_w, gla_gate_fw_b,
           gla_gate_bw_w, gla_gate_bw_b, gla_out_norm, ffn2_norm, ffn2_w_in, ffn2_w_out):
    bp, sp, _ = x_prompt.shape
    bs, ss, _ = x_sample.shape
    depth = w_in.shape[0]
    lay = _layout(bp, sp, bs, ss)
    tile_tab, smax = _tile_table(lay)
    tabs = _rope_tables(smax)
    h = _row_pieces(lay, {"p": (x_prompt.reshape(-1, D_MODEL), jnp.tile(meta_tokens, (bp, 1))),
                          "s": (x_sample.reshape(-1, D_MODEL), jnp.tile(meta_tokens, (bs, 1)))})
    wnorm = jnp.tile(gla_out_norm, (1, GLA_HEADS))
    n_mla = MLA_HEADS * MLA_V
    tq = {"p": min(TM, sp), "s": min(TM, ss)}

    def ffn_w(w_i, w_o):
        return w_i[:, :D_FF].astype(BF16), w_i[:, D_FF:].astype(BF16), w_o.astype(BF16)

    y = None
    for i in range(depth):
        h = _ffn_call(h, ffn1_norm[i][None, :], *ffn_w(ffn1_w_in[i], ffn1_w_out[i]), lay)
        consts = _mixer_consts(i, mix_norm, w_in, mla_q_norm, mla_w_uq, mla_kv_norm, mla_w_ukv, gqa_q_norm,
                               gqa_k_norm, gla_gate_fw_w, gla_gate_fw_b, gla_gate_bw_w, gla_gate_bw_b)
        qm, km, vmt, qg, kg, vgt, lq, lk, lv, gfw, gbw, lgate = _mixin_call(h, tile_tab, consts, tabs)
        o_mla, o_gqa, o_gla = {}, {}, {}
        for name in ("p", "s"):
            g = lay[name]
            o_mla[name] = (_attn_call(qm, km, vmt, g, gqa=False, meta_q=False, tq=tq[name]),
                           _attn_call(qm, km, vmt, g, gqa=False, meta_q=True, tq=None))
            o_gqa[name] = (_attn_call(qg, kg, vgt, g, gqa=True, meta_q=False, tq=tq[name]),
                           _attn_call(qg, kg, vgt, g, gqa=True, meta_q=True, tq=None))
            o_gla[name] = _gla_call(lq, lk, lv, gfw, gbw, lgate, wnorm[i][None, :], g)
        wo = w_out[i].astype(BF16)
        proj = (_row_pieces(lay, o_mla), _row_pieces(lay, o_gqa), _row_pieces(lay, o_gla),
                wo[:n_mla], wo[n_mla:2 * n_mla], wo[2 * n_mla:])
        last = i == depth - 1
        out = _ffn_call(h, ffn2_norm[i][None, :], *ffn_w(ffn2_w_in[i], ffn2_w_out[i]), lay, proj=proj,
                        final=final_norm[None, :] if last else None)
        if last:
            y = out
        else:
            h = out
    return y[0].reshape(bp, sp, D_MODEL), y[1].reshape(bs, ss, D_MODEL)
```

```python
import functools

import numpy as np
import jax
import jax.numpy as jnp
from jax import lax
from jax.experimental import pallas as pl
from jax.experimental.pallas import tpu as pltpu

F32 = jnp.float32
BF16 = jnp.bfloat16

D_MODEL = 1024
N_META = 16
GRID_W = 64
NORM_EPS = 1e-6
ROPE_THETA = 10000.0
D_FF = 2816
MLA_HEADS = 6
MLA_Q_LORA = 256
MLA_KV_LORA = 128
MLA_NOPE = 64
MLA_ROPE = 32
MLA_V = 64
GQA_HEADS = 6
GQA_KV_HEADS = 2
GQA_HEAD_DIM = 64
GLA_HEADS = 4
GLA_DK = 32
GLA_DV = 64
GLA_GATE_RANK = 16
GLA_GATE_NORM = 16.0
GLA_CHUNK = 64

LANES = 128
TM = 512
META_BLOCK = 128
HEAD_PAD = 128
VMEM_LIMIT = 56 * 1024 * 1024
FF_SPLITS = (0, 1536, D_FF)
ATTN_TK = 512
ATTN_MAX_EXCESS = 64.0
ATTN_UNROLL = 16
ATTN_STATIC_CHUNKS = 4

LOG2E = 1.4426950408889634
MLA_SCALE = (MLA_NOPE + MLA_ROPE) ** -0.5 * LOG2E
GQA_SCALE = GQA_HEAD_DIM ** -0.5 * LOG2E
SUM_ROWS = 16
VROWS_MLA = SUM_ROWS + MLA_V
VROWS_GQA = GQA_KV_HEADS * (SUM_ROWS + GQA_HEAD_DIM)
GLA_QSCALE = GLA_DK ** -0.5

C_CQ = 0
C_CKV = C_CQ + MLA_Q_LORA
C_KP = C_CKV + MLA_KV_LORA
C_KPR = C_KP + HEAD_PAD
C_GQ = C_KPR + HEAD_PAD
C_GQR = C_GQ + GQA_HEADS * HEAD_PAD
C_GK = C_GQR + GQA_HEADS * HEAD_PAD
C_GKR = C_GK + 128
C_LQ = C_GKR + 128
C_LK = C_LQ + 128
C_LV = C_LK + 128
C_LGATE = C_LV + 256
C_LG = C_LGATE + 256
C_END = C_LG + 128

NT_DIMS = (((1,), (1,)), ((), ()))
TN_DIMS = (((0,), (0,)), ((), ()))


def _dot(a, b):
    return jnp.dot(a, b, preferred_element_type=F32)


def _dot_nt(a, b):
    return lax.dot_general(a, b, NT_DIMS, preferred_element_type=F32)


def _dot_tn(a, b):
    return lax.dot_general(a, b, TN_DIMS, preferred_element_type=F32)


def _rms(x, w):
    return x * lax.rsqrt(jnp.mean(x * x, axis=-1, keepdims=True) + NORM_EPS) * w


def _params(sem):
    return pltpu.CompilerParams(dimension_semantics=sem, vmem_limit_bytes=VMEM_LIMIT)


def _const_spec(shape):
    nd = len(shape)
    return pl.BlockSpec(shape, lambda *_: (0,) * nd, pipeline_mode=pl.Buffered(1))


def _pick(i, refs, bounds):
    v = refs[-1][...]
    for k in reversed(range(len(refs) - 1)):
        v = jnp.where(i < bounds[k + 1], refs[k][...], v)
    return v


def _ffn_body(*refs, src_bounds, proj_bounds, has_final, tiles_p, tiles_s):
    it = iter(refs)
    take = lambda n: [next(it) for _ in range(n)]
    i = pl.program_id(0)
    h = _pick(i, take(len(src_bounds) - 1), src_bounds)
    if proj_bounds is not None:
        o_parts = [take(len(proj_bounds) - 1) for _ in range(3)]
        w_parts = take(3)
    nw_ref, wg_ref, wu_ref, wo_ref = take(4)
    if has_final:
        fw_ref, yp_ref, ys_ref = take(3)
    else:
        (out_ref,) = take(1)

    if proj_bounds is not None:
        for o_refs, w_ref in zip(o_parts, w_parts):
            h = h + _dot(_pick(i, o_refs, proj_bounds), w_ref[...])
    xn = _rms(h, nw_ref[...]).astype(BF16)
    acc = jnp.zeros(h.shape, F32)
    for lo, hi in zip(FF_SPLITS[:-1], FF_SPLITS[1:]):
        g = _dot(xn, wg_ref[:, lo:hi])
        u = _dot(xn, wu_ref[:, lo:hi])
        a = (g * jax.nn.sigmoid(g) * u).astype(BF16)
        acc = acc + _dot(a, wo_ref[lo:hi, :])
    h2 = h + 0.5 * acc
    if has_final:
        y = _rms(h2, fw_ref[...])

        @pl.when(i < tiles_p)
        def _():
            yp_ref[...] = y

        @pl.when(jnp.logical_and(i >= tiles_p, i < tiles_p + tiles_s))
        def _():
            ys_ref[...] = y
    else:
        out_ref[...] = h2


def _pieces(x):
    arrs = list(x) if isinstance(x, (tuple, list)) else [x]
    counts = [a.shape[0] // TM for a in arrs]
    bounds = [0]
    for c in counts:
        bounds.append(bounds[-1] + c)
    specs = [pl.BlockSpec((TM, a.shape[1]), lambda i, f=f, c=c: (jnp.clip(i - f, 0, c - 1), 0))
             for a, f, c in zip(arrs, bounds, counts)]
    return arrs, specs, tuple(bounds)


def _ffn_call(h, nw, wg, wu, wo, lay, proj=None, final=None):
    R = lay["R"]
    n_tiles = R // TM
    tiles_p, tiles_s = lay["p"]["main_rows"] // TM, lay["s"]["main_rows"] // TM
    ins, specs, src_bounds = _pieces(h)
    assert src_bounds[-1] == n_tiles
    proj_bounds = None
    if proj is not None:
        oa, ob, oc, wa, wb, wc = proj
        for o in (oa, ob, oc):
            arrs, sp, proj_bounds = _pieces(o)
            assert proj_bounds[-1] == n_tiles
            ins += arrs
            specs += sp
        ins += [wa, wb, wc]
        specs += [_const_spec(wa.shape), _const_spec(wb.shape), _const_spec(wc.shape)]
    ins += [nw, wg, wu, wo]
    specs += [_const_spec(nw.shape), _const_spec(wg.shape), _const_spec(wu.shape), _const_spec(wo.shape)]
    if final is not None:
        ins.append(final)
        specs.append(_const_spec(final.shape))
        out_shape = (jax.ShapeDtypeStruct((tiles_p * TM, D_MODEL), F32),
                     jax.ShapeDtypeStruct((tiles_s * TM, D_MODEL), F32))
        part = lambda first, count: pl.BlockSpec((TM, D_MODEL), lambda i: (jnp.clip(i - first, 0, count - 1), 0))
        out_specs = (part(0, tiles_p), part(tiles_p, tiles_s))
    else:
        out_shape = jax.ShapeDtypeStruct((R, D_MODEL), F32)
        out_specs = pl.BlockSpec((TM, D_MODEL), lambda i: (i, 0))
    body = functools.partial(_ffn_body, src_bounds=src_bounds, proj_bounds=proj_bounds, has_final=final is not None,
                             tiles_p=tiles_p, tiles_s=tiles_s)
    return pl.pallas_call(
        body, grid=(n_tiles,), in_specs=specs, out_specs=out_specs, out_shape=out_shape,
        compiler_params=_params(("arbitrary",)),
        name="ffn" + ("_proj" if proj is not None else "") + ("_final" if final is not None else ""),
    )(*ins)


def _mixin_body(tab_ref, h_ref, nw_ref, wbig_ref, cm_ref, sm_ref, cg_ref, sg_ref,
                qn_ref, kvn_ref, wqa_ref, wqb_ref, wka_ref, wvt_ref,
                gqc_ref, gqs_ref, gkc_ref, gks_ref, wgvt_ref, wg_ref, bg_ref,
                qm_ref, km_ref, vmt_ref, qg_ref, kg_ref, vgt_ref,
                lq_ref, lk_ref, lv_ref, gfw_ref, gbw_ref, lgate_ref):
    del tab_ref
    hn = _rms(h_ref[...], nw_ref[...]).astype(BF16)

    pall = _dot(hn, wbig_ref[...])

    def proj(a, width):
        return pall[:, a:a + width]

    cm, sm = cm_ref[...], sm_ref[...]
    cg, sg = cg_ref[...], sg_ref[...]

    cqn = _rms(proj(C_CQ, MLA_Q_LORA), qn_ref[...]).astype(BF16)
    qa = _dot(cqn, wqa_ref[...])
    qb = _dot(cqn, wqb_ref[...])
    for hd in range(MLA_HEADS):
        blk = slice(hd * HEAD_PAD, (hd + 1) * HEAD_PAD)
        qm_ref[:, blk] = ((qa[:, blk] * cm + qb[:, blk] * sm) * MLA_SCALE).astype(BF16)

    kvn = _rms(proj(C_CKV, MLA_KV_LORA), kvn_ref[...]).astype(BF16)
    kpe = proj(C_KP, HEAD_PAD) * cm + proj(C_KPR, HEAD_PAD) * sm
    ka = _dot(kvn, wka_ref[...])
    for hd in range(MLA_HEADS):
        blk = slice(hd * HEAD_PAD, (hd + 1) * HEAD_PAD)
        km_ref[:, blk] = (ka[:, blk] + kpe).astype(BF16)
    ones = jnp.ones((SUM_ROWS, hn.shape[0]), F32)
    vt = _dot_nt(wvt_ref[...], kvn)
    pieces = []
    for hd in range(MLA_HEADS):
        pieces += [ones, vt[hd * MLA_V:(hd + 1) * MLA_V]]
    vmt_ref[0] = jnp.concatenate(pieces, axis=0).astype(BF16)

    tqc, tqs = gqc_ref[...] * cg, gqs_ref[...] * sg
    for hd in range(GQA_HEADS):
        x = proj(C_GQ + hd * HEAD_PAD, HEAD_PAD)
        xr = proj(C_GQR + hd * HEAD_PAD, HEAD_PAD)
        n = lax.rsqrt(jnp.sum(x * x, axis=-1, keepdims=True) * (1.0 / GQA_HEAD_DIM) + NORM_EPS)
        qg_ref[:, hd * HEAD_PAD:(hd + 1) * HEAD_PAD] = (n * (x * tqc + xr * tqs) * GQA_SCALE).astype(BF16)
    x = proj(C_GK, 128)
    xr = proj(C_GKR, 128)
    lo = lax.broadcasted_iota(jnp.int32, x.shape, 1) < GQA_HEAD_DIM
    x2 = x * x
    ms0 = jnp.sum(jnp.where(lo, x2, 0.0), axis=-1, keepdims=True) * (1.0 / GQA_HEAD_DIM)
    ms1 = jnp.sum(jnp.where(lo, 0.0, x2), axis=-1, keepdims=True) * (1.0 / GQA_HEAD_DIM)
    n = jnp.where(lo, lax.rsqrt(ms0 + NORM_EPS), lax.rsqrt(ms1 + NORM_EPS))
    kg_ref[...] = (n * (x * (gkc_ref[...] * cg) + xr * (gks_ref[...] * sg))).astype(BF16)
    vg = _dot_nt(wgvt_ref[...], hn)
    vgt_ref[0] = jnp.concatenate([ones, vg[:GQA_HEAD_DIM], ones, vg[GQA_HEAD_DIM:]], axis=0).astype(BF16)

    lq_ref[...] = proj(C_LQ, 128) * GLA_QSCALE
    lk_ref[...] = proj(C_LK, 128)
    lv_ref[...] = proj(C_LV, 256)
    lgate_ref[...] = proj(C_LGATE, 256)
    gg = _dot(proj(C_LG, 128).astype(BF16), wg_ref[...]) + bg_ref[...]
    ls = (jnp.minimum(gg, 0.0) - jnp.log1p(jnp.exp(-jnp.abs(gg)))) * (1.0 / GLA_GATE_NORM)
    gfw_ref[...] = ls[:, :128]
    gbw_ref[...] = ls[:, 128:]


def _mixin_call(h, tile_tab, consts, tabs):
    R = h.shape[0]
    n_tiles = R // TM
    row = lambda w: pl.BlockSpec((TM, w), lambda i, t: (i, 0))
    tabspec = pl.BlockSpec((TM, LANES), lambda i, t: (t[i], 0))
    cs = lambda a: _const_spec(a.shape)
    col3 = lambda r: pl.BlockSpec((1, r, TM), lambda i, t: (i, 0, 0))
    ins = [h, consts["nw"], consts["wbig"], tabs["cm"], tabs["sm"], tabs["cg"], tabs["sg"],
           consts["qn"], consts["kvn"], consts["wqa"], consts["wqb"], consts["wka"], consts["wvt"],
           consts["gqc"], consts["gqs"], consts["gkc"], consts["gks"], consts["wgvt"],
           consts["wg"], consts["bg"]]
    specs = [row(D_MODEL), cs(ins[1]), cs(ins[2]), tabspec, tabspec, tabspec, tabspec] + [cs(a) for a in ins[7:]]
    W6 = MLA_HEADS * HEAD_PAD
    out_shape = (
        jax.ShapeDtypeStruct((R, W6), BF16), jax.ShapeDtypeStruct((R, W6), BF16),
        jax.ShapeDtypeStruct((n_tiles, MLA_HEADS * VROWS_MLA, TM), BF16),
        jax.ShapeDtypeStruct((R, W6), BF16), jax.ShapeDtypeStruct((R, 128), BF16),
        jax.ShapeDtypeStruct((n_tiles, VROWS_GQA, TM), BF16),
        jax.ShapeDtypeStruct((R, 128), F32), jax.ShapeDtypeStruct((R, 128), F32),
        jax.ShapeDtypeStruct((R, 256), F32), jax.ShapeDtypeStruct((R, 128), F32),
        jax.ShapeDtypeStruct((R, 128), F32), jax.ShapeDtypeStruct((R, 256), F32),
    )
    out_specs = (row(W6), row(W6), col3(MLA_HEADS * VROWS_MLA), row(W6), row(128), col3(VROWS_GQA),
                 row(128), row(128), row(256), row(128), row(128), row(256))
    gs = pltpu.PrefetchScalarGridSpec(num_scalar_prefetch=1, grid=(n_tiles,), in_specs=specs, out_specs=out_specs)
    return pl.pallas_call(_mixin_body, grid_spec=gs, out_shape=out_shape,
                          compiler_params=_params(("arbitrary",)), name="mixer_in")(tile_tab, *ins)


def _attn_body(q_ref, k_ref, kmeta_ref, vt_ref, vtmeta_ref, o_ref, *scratch,
               gqa, tq, n_chunks, meta_q, b_axis, p_axis, unroll):
    b = pl.program_id(b_axis)
    p = pl.program_id(p_axis)
    slot = b % (META_BLOCK // N_META)
    rows = lax.broadcasted_iota(jnp.int32, (META_BLOCK, 1), 0)
    meta_valid = (rows // N_META) == slot
    zeros = jnp.zeros((VROWS_MLA, tq), F32)

    def exact_update(carry, s_t, v_t):
        m, acc = carry
        m_new = jnp.maximum(m, jnp.max(s_t, axis=0, keepdims=True))
        p_t = jnp.exp2(s_t - m_new).astype(BF16)
        return m_new, jnp.exp2(m - m_new) * acc + _dot(v_t, p_t)

    def fast_update(carry, s_t, v_t):
        m, acc, excess = carry
        p_t = jnp.exp2(s_t - m).astype(BF16)
        cmax = jnp.max(s_t, axis=0, keepdims=True)
        m_new = jnp.maximum(m, cmax)
        return m_new, jnp.exp2(m - m_new) * (acc + _dot(v_t, p_t)), jnp.maximum(excess, cmax - m)

    def heads(fast):
        update = fast_update if fast else exact_update
        parts = TM // ATTN_TK
        q = [q_ref[:, i * HEAD_PAD:(i + 1) * HEAD_PAD] for i in range(2)]
        ks = [slice(0, HEAD_PAD) if gqa else slice(i * HEAD_PAD, (i + 1) * HEAD_PAD) for i in range(2)]
        if gqa:
            groups = [(2 * p + i) // (GQA_HEADS // GQA_KV_HEADS) for i in range(2)]
            vs = [pl.ds(pl.multiple_of(g * VROWS_MLA, SUM_ROWS), VROWS_MLA) for g in groups]
        else:
            vs = [pl.ds(i * VROWS_MLA, VROWS_MLA) for i in range(2)]

        def scores(i, c):
            return _dot_nt(k_ref[pl.ds(pl.multiple_of(c * ATTN_TK, ATTN_TK), ATTN_TK), ks[i]], q[i])

        def values(i, tile, part):
            return vt_ref[tile, vs[i], part * ATTN_TK:(part + 1) * ATTN_TK]

        def meta_scores(i):
            return jnp.where(meta_valid, _dot_nt(kmeta_ref[:, ks[i]], q[i]), -jnp.inf)

        def init(s0):
            if fast:
                return jnp.max(s0, axis=0, keepdims=True), zeros, jnp.zeros((1, tq), F32)
            return jnp.full((1, tq), -jnp.inf, F32), zeros

        if n_chunks <= ATTN_STATIC_CHUNKS:
            s_cur = [scores(i, 0) for i in range(2)]
            carry = [init(s) for s in s_cur]
            for c in range(n_chunks):
                for i in range(2):
                    s_next = scores(i, c + 1) if c + 1 < n_chunks else meta_scores(i)
                    carry[i] = update(carry[i], s_cur[i], values(i, c // parts, c % parts))
                    s_cur[i] = s_next
            return [update(carry[i], s_cur[i], vtmeta_ref[0, vs[i], :]) for i in range(2)]

        s_buf = scratch[-1]
        carry = []
        for i in range(2):
            s0 = scores(i, 0)
            s_buf[i, 0] = s0
            carry.append(init(s0))

        def group_step(j, st):
            st = list(st)
            for u in range(unroll):
                c = j * unroll + u
                for i in range(2):
                    s_buf[i, (u + 1) % 2] = scores(i, jnp.minimum(c + 1, n_chunks - 1))
                    st[i] = update(st[i], s_buf[i, u % 2], values(i, j * (unroll // parts) + u // parts, u % parts))
            return tuple(st)

        carry = lax.fori_loop(0, n_chunks // unroll, group_step, tuple(carry))
        return [update(carry[i], meta_scores(i), vtmeta_ref[0, vs[i], :]) for i in range(2)]

    if n_chunks <= ATTN_STATIC_CHUNKS:
        fast = heads(True)
        worst = jnp.max(jnp.maximum(fast[0][2], fast[1][2]))
        accs = lax.cond(worst > ATTN_MAX_EXCESS,
                        lambda: tuple(c[1] for c in heads(False)),
                        lambda: tuple(c[1] for c in fast))
    else:
        accs = [c[1] for c in heads(False)]
    halves = [acc[SUM_ROWS:] / acc[0:1] for acc in accs]
    o = jnp.concatenate(halves, axis=0).T.astype(BF16)
    if meta_q:
        stage = scratch[0]

        @pl.when(slot == 0)
        def _():
            o_ref[...] = jnp.zeros(o_ref.shape, o_ref.dtype)

        stage[...] = o
        off = pl.multiple_of(slot * N_META, N_META)
        o_ref[pl.ds(off, N_META), :] = stage[pl.ds(off, N_META), :]
    else:
        o_ref[...] = o


def _attn_call(q, k, vt, g, *, gqa, meta_q, tq):
    B, S = g["B"], g["S"]
    n_chunks = S // ATTN_TK
    n_tiles = S // TM
    kw = HEAD_PAD if gqa else 2 * HEAD_PAD
    vrows = VROWS_GQA if gqa else 2 * VROWS_MLA
    main_blk = g["main_off"] // S
    meta_blk = g["meta_off"] // META_BLOCK
    per_blk = META_BLOCK // N_META
    pk = (lambda p: 0) if gqa else (lambda p: p)
    if meta_q:
        grid = (3, B)
        ax = lambda f: (lambda p, b: f(b, p, 0))
        tq = META_BLOCK
        q_map = lambda b, p, t: (meta_blk + b // per_blk, p)
        o_map = lambda b, p, t: (b // per_blk, p)
        out_rows = pl.cdiv(B * N_META, META_BLOCK) * META_BLOCK
        sem = ("arbitrary", "arbitrary")
        scratch = [pltpu.VMEM((META_BLOCK, HEAD_PAD), BF16)]
        b_axis, p_axis = 1, 0
    else:
        nq = S // tq
        grid = (B, 3, nq)
        ax = lambda f: f
        q_map = lambda b, p, t: ((g["main_off"] + b * S) // tq + t, p)
        o_map = lambda b, p, t: (b * nq + t, p)
        out_rows = B * S
        sem = ("arbitrary", "arbitrary", "arbitrary")
        scratch = []
        b_axis, p_axis = 0, 1
    in_specs = [
        pl.BlockSpec((tq, 2 * HEAD_PAD), ax(q_map)),
        pl.BlockSpec((S, kw), ax(lambda b, p, t: (main_blk + b, pk(p)))),
        pl.BlockSpec((META_BLOCK, kw), ax(lambda b, p, t: (meta_blk + b // per_blk, pk(p)))),
        pl.BlockSpec((n_tiles, vrows, TM), ax(lambda b, p, t: (main_blk + b, pk(p), 0))),
        pl.BlockSpec((1, vrows, META_BLOCK),
                     ax(lambda b, p, t: (g["meta_off"] // TM + (b * N_META) // TM, pk(p),
                                         ((b * N_META) % TM) // META_BLOCK))),
    ]
    unroll = None
    if n_chunks > ATTN_STATIC_CHUNKS:
        step = 2 * TM // ATTN_TK
        unroll = max(u for u in range(step, ATTN_UNROLL + 1, step) if n_chunks % u == 0)
        scratch = scratch + [pltpu.VMEM((2, 2, ATTN_TK, tq), F32)]
    body = functools.partial(_attn_body, gqa=gqa, tq=tq, n_chunks=n_chunks, meta_q=meta_q,
                             b_axis=b_axis, p_axis=p_axis, unroll=unroll)
    return pl.pallas_call(
        body, grid=grid, in_specs=in_specs,
        out_specs=pl.BlockSpec((tq, HEAD_PAD), ax(o_map)),
        out_shape=jax.ShapeDtypeStruct((out_rows, 3 * HEAD_PAD), BF16),
        scratch_shapes=scratch, compiler_params=_params(sem),
        name=("gqa" if gqa else "mla") + ("_metaq_" if meta_q else "_") + g["name"],
    )(q, k, k, vt, vt)


def _gla_consts():
    c = GLA_CHUNK
    r = lax.broadcasted_iota(jnp.int32, (c, c), 0)
    s = lax.broadcasted_iota(jnp.int32, (c, c), 1)
    r4 = lax.broadcasted_iota(jnp.int32, (GLA_HEADS * c, c), 0) % c
    s4 = lax.broadcasted_iota(jnp.int32, (GLA_HEADS * c, c), 1)
    kl = lax.broadcasted_iota(jnp.int32, (1, GLA_HEADS * GLA_DK), 1) // GLA_DK
    vl = lax.broadcasted_iota(jnp.int32, (1, GLA_HEADS * GLA_DV), 1) // GLA_DV
    kr = lax.broadcasted_iota(jnp.int32, (GLA_HEADS * GLA_DK, GLA_HEADS * GLA_DV), 0) // GLA_DK
    vc = lax.broadcasted_iota(jnp.int32, (GLA_HEADS * GLA_DK, GLA_HEADS * GLA_DV), 1) // GLA_DV
    return dict(
        tril=(s <= r).astype(BF16), triu=(s >= r).astype(BF16),
        mask4_fw=s4 <= r4, mask4_bw=s4 >= r4,
        kmask=[(kl == h).astype(F32) for h in range(GLA_HEADS)],
        vmask=[(vl == h).astype(F32) for h in range(GLA_HEADS)],
        blockdiag=(kr == vc).astype(F32),
        ones=jnp.ones((c, GLA_HEADS * GLA_DV), BF16),
    )


def _gla_chunks(qs, ks, vs, gs, state, cst, fwd):
    c, n = GLA_CHUNK, len(qs)
    nk = GLA_HEADS * GLA_DK
    tri = cst["tril"] if fwd else cst["triu"]
    hilo = []
    for g in gs:
        hi = g.astype(BF16)
        hilo.append(jnp.concatenate([hi, (g - hi.astype(F32)).astype(BF16)], axis=1))
    bcum = [_dot(tri, x) for x in hilo]
    bcum = [x[:, :nk] + x[:, nk:] for x in bcum]
    ltot = [_dot_tn(x, cst["ones"]) for x in hilo]
    decay = [jnp.exp(x[:nk] + x[nk:]) for x in ltot]
    qe = [q * jnp.exp(b) for q, b in zip(qs, bcum)]
    ke = [(k * jnp.exp(-b)).astype(BF16) for k, b in zip(ks, bcum)]
    kd = [(k * jnp.exp((b[c - 1:c] if fwd else b[0:1]) - b)).astype(BF16) for k, b in zip(ks, bcum)]
    vb = [v.astype(BF16) for v in vs]
    qstack = [jnp.concatenate([x * cst["kmask"][h] for h in range(GLA_HEADS)], axis=0).astype(BF16) for x in qe]
    mask = cst["mask4_fw"] if fwd else cst["mask4_bw"]
    att = [jnp.where(mask, _dot_nt(x, y), 0.0).astype(BF16) for x, y in zip(qstack, ke)]
    full = [_dot(x, y) for x, y in zip(att, vb)]
    dstate = [_dot_tn(x, y) * cst["blockdiag"] for x, y in zip(kd, vb)]
    states = []
    for i in range(n):
        states.append(state.astype(BF16))
        state = decay[i] * state + dstate[i]
    outs = []
    for i in range(n):
        o = _dot(qe[i].astype(BF16), states[i])
        for h in range(GLA_HEADS):
            o = o + full[i][h * c:(h + 1) * c] * cst["vmask"][h]
        outs.append(o)
    return outs, state


def _meta_chunk(ref):
    x = ref[...]
    return jnp.concatenate([jnp.zeros((GLA_CHUNK - N_META, x.shape[1]), x.dtype), x], axis=0)


def _gla_fw_body(q_ref, k_ref, v_ref, g_ref, qm_ref, km_ref, vm_ref, gm_ref, o_ref, om_ref, state):
    cst = _gla_consts()

    @pl.when(pl.program_id(1) == 0)
    def _():
        (o,), s = _gla_chunks([_meta_chunk(qm_ref)], [_meta_chunk(km_ref)], [_meta_chunk(vm_ref)],
                              [_meta_chunk(gm_ref)], jnp.zeros(state.shape, F32), cst, True)
        om_ref[...] = o[GLA_CHUNK - N_META:]
        state[...] = s

    sls = [slice(c * GLA_CHUNK, (c + 1) * GLA_CHUNK) for c in range(TM // GLA_CHUNK)]
    outs, s = _gla_chunks([q_ref[sl, :] for sl in sls], [k_ref[sl, :] for sl in sls], [v_ref[sl, :] for sl in sls],
                          [g_ref[sl, :] for sl in sls], state[...], cst, True)
    for sl, o in zip(sls, outs):
        o_ref[sl, :] = o
    state[...] = s


def _gla_finish(o, gate, w, bd):
    x2 = o * o
    hi = x2.astype(BF16)
    lo = (x2 - hi.astype(F32)).astype(BF16)
    ms = _dot(hi, bd) + _dot(lo, bd)
    return (o * lax.rsqrt(ms + NORM_EPS) * w * (gate * jax.nn.sigmoid(gate))).astype(BF16)


def _gla_bw_body(q_ref, k_ref, v_ref, g_ref, ofw_ref, gate_ref, qm_ref, km_ref, vm_ref, gm_ref, ofwm_ref, gatem_ref,
                 w_ref, o_ref, om_ref, state):
    cst = _gla_consts()
    t = pl.program_id(1)
    r = lax.broadcasted_iota(jnp.int32, (GLA_HEADS * GLA_DV,) * 2, 0) // GLA_DV
    c_ = lax.broadcasted_iota(jnp.int32, (GLA_HEADS * GLA_DV,) * 2, 1) // GLA_DV
    bd = jnp.where(r == c_, 1.0 / GLA_DV, 0.0).astype(BF16)

    @pl.when(t == 0)
    def _():
        state[...] = jnp.zeros(state.shape, F32)

    sls = [slice(c * GLA_CHUNK, (c + 1) * GLA_CHUNK) for c in reversed(range(TM // GLA_CHUNK))]
    outs, s = _gla_chunks([q_ref[sl, :] for sl in sls], [k_ref[sl, :] for sl in sls], [v_ref[sl, :] for sl in sls],
                          [g_ref[sl, :] for sl in sls], state[...], cst, False)
    for sl, o in zip(sls, outs):
        o_ref[sl, :] = _gla_finish(o + ofw_ref[sl, :], gate_ref[sl, :], w_ref[...], bd)
    state[...] = s

    @pl.when(t == pl.num_programs(1) - 1)
    def _():
        (o,), _ = _gla_chunks([_meta_chunk(qm_ref)], [_meta_chunk(km_ref)], [_meta_chunk(vm_ref)],
                              [_meta_chunk(gm_ref)], s, cst, False)
        om_ref[...] = _gla_finish(o[GLA_CHUNK - N_META:] + ofwm_ref[...], gatem_ref[...], w_ref[...], bd)


def _gla_call(lq, lk, lv, gfw, gbw, lgate, wnorm, g):
    B, S = g["B"], g["S"]
    nt = S // TM
    main0 = g["main_off"] // TM
    meta0 = g["meta_off"] // N_META
    mrow = lambda w, rev: pl.BlockSpec(
        (TM, w), (lambda b, t: (main0 + b * nt + (nt - 1 - t), 0)) if rev else (lambda b, t: (main0 + b * nt + t, 0)))
    meta = lambda w: pl.BlockSpec((N_META, w), lambda b, t: (meta0 + b, 0))
    orow = lambda w, rev: pl.BlockSpec(
        (TM, w), (lambda b, t: (b * nt + (nt - 1 - t), 0)) if rev else (lambda b, t: (b * nt + t, 0)))
    ometa = lambda w: pl.BlockSpec((N_META, w), lambda b, t: (b, 0))
    sem = _params(("arbitrary", "arbitrary"))
    state = [pltpu.VMEM((GLA_HEADS * GLA_DK, GLA_HEADS * GLA_DV), F32)]
    ofw, ofw_meta = pl.pallas_call(
        _gla_fw_body, grid=(B, nt),
        in_specs=[mrow(128, False), mrow(128, False), mrow(256, False), mrow(128, False),
                  meta(128), meta(128), meta(256), meta(128)],
        out_specs=(orow(256, False), ometa(256)),
        out_shape=(jax.ShapeDtypeStruct((B * S, 256), F32), jax.ShapeDtypeStruct((B * N_META, 256), F32)),
        scratch_shapes=state, compiler_params=sem, name="gla_fw_" + g["name"],
    )(lq, lk, lv, gfw, lq, lk, lv, gfw)
    o, o_meta = pl.pallas_call(
        _gla_bw_body, grid=(B, nt),
        in_specs=[mrow(128, True), mrow(128, True), mrow(256, True), mrow(128, True), orow(256, True), mrow(256, True),
                  meta(128), meta(128), meta(256), meta(128), ometa(256), meta(256), _const_spec(wnorm.shape)],
        out_specs=(orow(256, True), ometa(256)),
        out_shape=(jax.ShapeDtypeStruct((B * S, 256), BF16), jax.ShapeDtypeStruct((B * N_META, 256), BF16)),
        scratch_shapes=state, compiler_params=sem, name="gla_bw_" + g["name"],
    )(lq, lk, lv, gbw, ofw, lgate, lq, lk, lv, gbw, ofw_meta, lgate, wnorm)
    return o, o_meta


def _rot_cols(w, n):
    s = w.shape
    w4 = w.reshape(s[:-1] + (s[-1] // (2 * n), 2, n))
    return jnp.stack([-w4[..., 1, :], w4[..., 0, :]], axis=-2).reshape(s)


def _swap_halves(w, n):
    w3 = w.reshape(-1, 2, n)
    return w3[:, ::-1, :].reshape(w.shape)


def _layout(bp, sp, bs, ss):
    up = lambda x: -(-x // TM) * TM
    lay = {"p": dict(name="p", B=bp, S=sp, main_off=0, main_rows=bp * sp),
           "s": dict(name="s", B=bs, S=ss, main_off=bp * sp, main_rows=bs * ss)}
    lay["p"]["meta_off"] = bp * sp + bs * ss
    lay["s"]["meta_off"] = lay["p"]["meta_off"] + up(bp * N_META)
    lay["R"] = lay["s"]["meta_off"] + up(bs * N_META)
    for g in (lay["p"], lay["s"]):
        assert g["S"] % TM == 0 and g["main_off"] % g["S"] == 0 and g["S"] % GRID_W == 0
    return lay


def _tile_table(lay):
    smax = max(lay["p"]["S"], lay["s"]["S"])
    tab = []
    for g in (lay["p"], lay["s"]):
        tab += [t % (g["S"] // TM) for t in range(g["main_rows"] // TM)]
    tab += [smax // TM] * ((lay["R"] - lay["p"]["meta_off"]) // TM)
    return jnp.asarray(np.asarray(tab, np.int32)), smax


def _rope_tables(smax):
    r = jnp.arange(smax, dtype=jnp.int32)
    meta = jnp.arange(TM, dtype=jnp.int32) % N_META
    zero = jnp.zeros((TM,), jnp.int32)
    pos = jnp.concatenate([r + N_META, meta]).astype(F32)
    row = jnp.concatenate([r // GRID_W, zero]).astype(F32)
    col = jnp.concatenate([r % GRID_W, zero]).astype(F32)
    inv_m = ROPE_THETA ** (-jnp.arange(0, MLA_ROPE, 2, dtype=F32) / MLA_ROPE)
    half = GQA_HEAD_DIM // 2
    inv_g = ROPE_THETA ** (-jnp.arange(0, half, 2, dtype=F32) / half)
    a1, ar, ac = pos[:, None] * inv_m, row[:, None] * inv_g, col[:, None] * inv_g
    n = pos.shape[0]
    one, zer = jnp.ones((n, MLA_NOPE), F32), jnp.zeros((n, MLA_NOPE), F32)
    pad = jnp.zeros((n, HEAD_PAD - MLA_NOPE - MLA_ROPE), F32)
    cm = jnp.concatenate([one, jnp.cos(a1), jnp.cos(a1), pad], axis=1)
    sm = jnp.concatenate([zer, jnp.sin(a1), jnp.sin(a1), pad], axis=1)
    cg = jnp.concatenate([jnp.cos(ar), jnp.cos(ar), jnp.cos(ac), jnp.cos(ac)] * 2, axis=1)
    sg = jnp.concatenate([jnp.sin(ar), jnp.sin(ar), jnp.sin(ac), jnp.sin(ac)] * 2, axis=1)
    return dict(cm=cm, sm=sm, cg=cg, sg=sg)


def _mixer_consts(i, mix_norm, w_in, mla_q_norm, mla_w_uq, mla_kv_norm, mla_w_ukv, gqa_q_norm, gqa_k_norm,
                  gate_fw_w, gate_fw_b, gate_bw_w, gate_bw_b):
    w = w_in[i]
    z = lambda n: jnp.zeros((D_MODEL, n), F32)
    o = np.cumsum([0, MLA_Q_LORA, MLA_KV_LORA, MLA_ROPE, 384, 128, 128, 128, 128, 256, 16, 16, 256])
    seg = lambda j: w[:, o[j]:o[j + 1]]
    w_kr, w_gq, w_gk = seg(2), seg(3), seg(4)
    w_gqr = _rot_cols(w_gq, 16)
    tail = HEAD_PAD - MLA_NOPE - MLA_ROPE

    def gq_blocks(m):
        out = []
        for h in range(GQA_HEADS):
            blk = m[:, h * GQA_HEAD_DIM:(h + 1) * GQA_HEAD_DIM]
            out += [blk, z(GQA_HEAD_DIM)] if h < GQA_HEADS // GQA_KV_HEADS else [z(GQA_HEAD_DIM), blk]
        return out

    wbig = jnp.concatenate(
        [seg(0), seg(1), z(MLA_NOPE), w_kr, z(tail), z(MLA_NOPE), _rot_cols(w_kr, 16), z(tail)]
        + gq_blocks(w_gq) + gq_blocks(w_gqr)
        + [w_gk, _rot_cols(w_gk, 16), seg(6), seg(7), seg(8), seg(11), seg(9), seg(10), z(128 - 2 * GLA_GATE_RANK)],
        axis=1).astype(BF16)
    assert wbig.shape[1] == C_END

    uq = mla_w_uq[i].reshape(MLA_Q_LORA, MLA_HEADS, MLA_NOPE + MLA_ROPE)
    zq = jnp.zeros((MLA_Q_LORA, MLA_HEADS, tail), F32)
    wqa = jnp.concatenate([uq, zq], axis=2).reshape(MLA_Q_LORA, -1).astype(BF16)
    wqb = jnp.concatenate([jnp.zeros((MLA_Q_LORA, MLA_HEADS, MLA_NOPE), F32),
                           _rot_cols(uq[:, :, MLA_NOPE:], 16), zq], axis=2).reshape(MLA_Q_LORA, -1).astype(BF16)
    ukv = mla_w_ukv[i].reshape(MLA_KV_LORA, MLA_HEADS, MLA_NOPE + MLA_V)
    wka = jnp.concatenate([ukv[:, :, :MLA_NOPE], jnp.zeros((MLA_KV_LORA, MLA_HEADS, HEAD_PAD - MLA_NOPE), F32)],
                          axis=2).reshape(MLA_KV_LORA, -1).astype(BF16)
    wvt = ukv[:, :, MLA_NOPE:].reshape(MLA_KV_LORA, -1).T.astype(BF16)
    wgvt = seg(5).T.astype(BF16)

    gq, gk = gqa_q_norm[i], gqa_k_norm[i]
    gqs, gks = _swap_halves(gq, 16), _swap_halves(gk, 16)
    two = lambda v: jnp.concatenate([v, v])[None, :]
    wg = jnp.zeros((128, 256), F32)
    wg = wg.at[:GLA_GATE_RANK, :128].set(gate_fw_w[i]).at[GLA_GATE_RANK:2 * GLA_GATE_RANK, 128:].set(gate_bw_w[i])
    return dict(
        nw=mix_norm[i][None, :], wbig=wbig, qn=mla_q_norm[i][None, :], kvn=mla_kv_norm[i][None, :],
        wqa=wqa, wqb=wqb, wka=wka, wvt=wvt, gqc=two(gq), gqs=two(gqs), gkc=two(gk), gks=two(gks),
        wgvt=wgvt, wg=wg.astype(BF16), bg=jnp.concatenate([gate_fw_b[i], gate_bw_b[i]])[None, :])


def _row_pieces(lay, parts):
    pieces = [parts["p"][0], parts["s"][0]]
    for name, nxt in (("p", lay["s"]["meta_off"]), ("s", lay["R"])):
        m = parts[name][1][:lay[name]["B"] * N_META]
        pad = nxt - lay[name]["meta_off"] - m.shape[0]
        pieces.append(jnp.concatenate([m, jnp.zeros((pad, m.shape[1]), m.dtype)], axis=0))
    return pieces


def kernel(x_prompt, x_sample, meta_tokens, final_norm, ffn1_norm, ffn1_w_in, ffn1_w_out, mix_norm, w_in, w_out,
           mla_q_norm, mla_w_uq, mla_kv_norm, mla_w_ukv, gqa_q_norm, gqa_k_norm, gla_gate_fw_w, gla_gate_fw_b,
           gla_gate_bw_w, gla_gate_bw_b, gla_out_norm, ffn2_norm, ffn2_w_in, ffn2_w_out):
    bp, sp, _ = x_prompt.shape
    bs, ss, _ = x_sample.shape
    depth = w_in.shape[0]
    lay = _layout(bp, sp, bs, ss)
    tile_tab, smax = _tile_table(lay)
    tabs = _rope_tables(smax)
    h = _row_pieces(lay, {"p": (x_prompt.reshape(-1, D_MODEL), jnp.tile(meta_tokens, (bp, 1))),
                          "s": (x_sample.reshape(-1, D_MODEL), jnp.tile(meta_tokens, (bs, 1)))})
    wnorm = jnp.tile(gla_out_norm, (1, GLA_HEADS))
    n_mla = MLA_HEADS * MLA_V
    tq = {"p": min(TM, sp), "s": min(TM, ss)}

    def ffn_w(w_i, w_o):
        return w_i[:, :D_FF].astype(BF16), w_i[:, D_FF:].astype(BF16), w_o.astype(BF16)

    y = None
    for i in range(depth):
        h = _ffn_call(h, ffn1_norm[i][None, :], *ffn_w(ffn1_w_in[i], ffn1_w_out[i]), lay)
        consts = _mixer_consts(i, mix_norm, w_in, mla_q_norm, mla_w_uq, mla_kv_norm, mla_w_ukv, gqa_q_norm,
                               gqa_k_norm, gla_gate_fw_w, gla_gate_fw_b, gla_gate_bw_w, gla_gate_bw_b)
        qm, km, vmt, qg, kg, vgt, lq, lk, lv, gfw, gbw, lgate = _mixin_call(h, tile_tab, consts, tabs)
        o_mla, o_gqa, o_gla = {}, {}, {}
        for name in ("p", "s"):
            g = lay[name]
            o_mla[name] = (_attn_call(qm, km, vmt, g, gqa=False, meta_q=False, tq=tq[name]),
                           _attn_call(qm, km, vmt, g, gqa=False, meta_q=True, tq=None))
            o_gqa[name] = (_attn_call(qg, kg, vgt, g, gqa=True, meta_q=False, tq=tq[name]),
                           _attn_call(qg, kg, vgt, g, gqa=True, meta_q=True, tq=None))
            o_gla[name] = _gla_call(lq, lk, lv, gfw, gbw, lgate, wnorm[i][None, :], g)
        wo = w_out[i].astype(BF16)
        proj = (_row_pieces(lay, o_mla), _row_pieces(lay, o_gqa), _row_pieces(lay, o_gla),
                wo[:n_mla], wo[n_mla:2 * n_mla], wo[2 * n_mla:])
        last = i == depth - 1
        out = _ffn_call(h, ffn2_norm[i][None, :], *ffn_w(ffn2_w_in[i], ffn2_w_out[i]), lay, proj=proj,
                        final=final_norm[None, :] if last else None)
        if last:
            y = out
        else:
            h = out
    return y[0].reshape(bp, sp, D_MODEL), y[1].reshape(bs, ss, D_MODEL)
```

```python
import functools

import numpy as np
import jax
import jax.numpy as jnp
from jax import lax
from jax.experimental import pallas as pl
from jax.experimental.pallas import tpu as pltpu

F32 = jnp.float32
BF16 = jnp.bfloat16

D_MODEL = 1024
N_META = 16
GRID_W = 64
NORM_EPS = 1e-6
ROPE_THETA = 10000.0
D_FF = 2816
MLA_HEADS = 6
MLA_Q_LORA = 256
MLA_KV_LORA = 128
MLA_NOPE = 64
MLA_ROPE = 32
MLA_V = 64
GQA_HEADS = 6
GQA_KV_HEADS = 2
GQA_HEAD_DIM = 64
GLA_HEADS = 4
GLA_DK = 32
GLA_DV = 64
GLA_GATE_RANK = 16
GLA_GATE_NORM = 16.0
GLA_CHUNK = 64

LANES = 128
TM = 512
META_BLOCK = 128
HEAD_PAD = 128
VMEM_LIMIT = 56 * 1024 * 1024
FF_SPLITS = (0, 1536, D_FF)
ATTN_TK = 512
ATTN_MAX_EXCESS = 64.0
ATTN_UNROLL = 16
ATTN_STATIC_CHUNKS = 4

LOG2E = 1.4426950408889634
MLA_SCALE = (MLA_NOPE + MLA_ROPE) ** -0.5 * LOG2E
GQA_SCALE = GQA_HEAD_DIM ** -0.5 * LOG2E
SUM_ROWS = 16
VROWS_MLA = SUM_ROWS + MLA_V
VROWS_GQA = GQA_KV_HEADS * (SUM_ROWS + GQA_HEAD_DIM)
GLA_QSCALE = GLA_DK ** -0.5

C_CQ = 0
C_CKV = C_CQ + MLA_Q_LORA
C_KP = C_CKV + MLA_KV_LORA
C_KPR = C_KP + HEAD_PAD
C_GQ = C_KPR + HEAD_PAD
C_GQR = C_GQ + GQA_HEADS * HEAD_PAD
C_GK = C_GQR + GQA_HEADS * HEAD_PAD
C_GKR = C_GK + 128
C_LQ = C_GKR + 128
C_LK = C_LQ + 128
C_LV = C_LK + 128
C_LGATE = C_LV + 256
C_LG = C_LGATE + 256
C_END = C_LG + 128

NT_DIMS = (((1,), (1,)), ((), ()))
TN_DIMS = (((0,), (0,)), ((), ()))


def _dot(a, b):
    return jnp.dot(a, b, preferred_element_type=F32)


def _dot_nt(a, b):
    return lax.dot_general(a, b, NT_DIMS, preferred_element_type=F32)


def _dot_tn(a, b):
    return lax.dot_general(a, b, TN_DIMS, preferred_element_type=F32)


def _rms(x, w):
    return x * lax.rsqrt(jnp.mean(x * x, axis=-1, keepdims=True) + NORM_EPS) * w


def _params(sem):
    return pltpu.CompilerParams(dimension_semantics=sem, vmem_limit_bytes=VMEM_LIMIT)


def _const_spec(shape):
    nd = len(shape)
    return pl.BlockSpec(shape, lambda *_: (0,) * nd, pipeline_mode=pl.Buffered(1))


def _pick(i, refs, bounds):
    v = refs[-1][...]
    for k in reversed(range(len(refs) - 1)):
        v = jnp.where(i < bounds[k + 1], refs[k][...], v)
    return v


def _ffn_body(*refs, src_bounds, proj_bounds, has_final, tiles_p, tiles_s):
    it = iter(refs)
    take = lambda n: [next(it) for _ in range(n)]
    i = pl.program_id(0)
    h = _pick(i, take(len(src_bounds) - 1), src_bounds)
    if proj_bounds is not None:
        o_parts = [take(len(proj_bounds) - 1) for _ in range(3)]
        w_parts = take(3)
    nw_ref, wg_ref, wu_ref, wo_ref = take(4)
    if has_final:
        fw_ref, yp_ref, ys_ref = take(3)
    else:
        (out_ref,) = take(1)

    if proj_bounds is not None:
        for o_refs, w_ref in zip(o_parts, w_parts):
            h = h + _dot(_pick(i, o_refs, proj_bounds), w_ref[...])
    xn = _rms(h, nw_ref[...]).astype(BF16)
    acc = jnp.zeros(h.shape, F32)
    for lo, hi in zip(FF_SPLITS[:-1], FF_SPLITS[1:]):
        g = _dot(xn, wg_ref[:, lo:hi])
        u = _dot(xn, wu_ref[:, lo:hi])
        a = (g * jax.nn.sigmoid(g) * u).astype(BF16)
        acc = acc + _dot(a, wo_ref[lo:hi, :])
    h2 = h + 0.5 * acc
    if has_final:
        y = _rms(h2, fw_ref[...])

        @pl.when(i < tiles_p)
        def _():
            yp_ref[...] = y

        @pl.when(jnp.logical_and(i >= tiles_p, i < tiles_p + tiles_s))
        def _():
            ys_ref[...] = y
    else:
        out_ref[...] = h2


def _pieces(x):
    arrs = list(x) if isinstance(x, (tuple, list)) else [x]
    counts = [a.shape[0] // TM for a in arrs]
    bounds = [0]
    for c in counts:
        bounds.append(bounds[-1] + c)
    specs = [pl.BlockSpec((TM, a.shape[1]), lambda i, f=f, c=c: (jnp.clip(i - f, 0, c - 1), 0))
             for a, f, c in zip(arrs, bounds, counts)]
    return arrs, specs, tuple(bounds)


def _ffn_call(h, nw, wg, wu, wo, lay, proj=None, final=None):
    R = lay["R"]
    n_tiles = R // TM
    tiles_p, tiles_s = lay["p"]["main_rows"] // TM, lay["s"]["main_rows"] // TM
    ins, specs, src_bounds = _pieces(h)
    assert src_bounds[-1] == n_tiles
    proj_bounds = None
    if proj is not None:
        oa, ob, oc, wa, wb, wc = proj
        for o in (oa, ob, oc):
            arrs, sp, proj_bounds = _pieces(o)
            assert proj_bounds[-1] == n_tiles
            ins += arrs
            specs += sp
        ins += [wa, wb, wc]
        specs += [_const_spec(wa.shape), _const_spec(wb.shape), _const_spec(wc.shape)]
    ins += [nw, wg, wu, wo]
    specs += [_const_spec(nw.shape), _const_spec(wg.shape), _const_spec(wu.shape), _const_spec(wo.shape)]
    if final is not None:
        ins.append(final)
        specs.append(_const_spec(final.shape))
        out_shape = (jax.ShapeDtypeStruct((tiles_p * TM, D_MODEL), F32),
                     jax.ShapeDtypeStruct((tiles_s * TM, D_MODEL), F32))
        part = lambda first, count: pl.BlockSpec((TM, D_MODEL), lambda i: (jnp.clip(i - first, 0, count - 1), 0))
        out_specs = (part(0, tiles_p), part(tiles_p, tiles_s))
    else:
        out_shape = jax.ShapeDtypeStruct((R, D_MODEL), F32)
        out_specs = pl.BlockSpec((TM, D_MODEL), lambda i: (i, 0))
    body = functools.partial(_ffn_body, src_bounds=src_bounds, proj_bounds=proj_bounds, has_final=final is not None,
                             tiles_p=tiles_p, tiles_s=tiles_s)
    return pl.pallas_call(
        body, grid=(n_tiles,), in_specs=specs, out_specs=out_specs, out_shape=out_shape,
        compiler_params=_params(("arbitrary",)),
        name="ffn" + ("_proj" if proj is not None else "") + ("_final" if final is not None else ""),
    )(*ins)


def _mixin_body(tab_ref, h_ref, nw_ref, wbig_ref, cm_ref, sm_ref, cg_ref, sg_ref,
                qn_ref, kvn_ref, wqa_ref, wqb_ref, wka_ref, wvt_ref,
                gqc_ref, gqs_ref, gkc_ref, gks_ref, wgvt_ref, wg_ref, bg_ref,
                qm_ref, km_ref, vmt_ref, qg_ref, kg_ref, vgt_ref,
                lq_ref, lk_ref, lv_ref, gfw_ref, gbw_ref, lgate_ref):
    del tab_ref
    hn = _rms(h_ref[...], nw_ref[...]).astype(BF16)

    pall = _dot(hn, wbig_ref[...])

    def proj(a, width):
        return pall[:, a:a + width]

    cm, sm = cm_ref[...], sm_ref[...]
    cg, sg = cg_ref[...], sg_ref[...]

    cqn = _rms(proj(C_CQ, MLA_Q_LORA), qn_ref[...]).astype(BF16)
    qa = _dot(cqn, wqa_ref[...])
    qb = _dot(cqn, wqb_ref[...])
    for hd in range(MLA_HEADS):
        blk = slice(hd * HEAD_PAD, (hd + 1) * HEAD_PAD)
        qm_ref[:, blk] = ((qa[:, blk] * cm + qb[:, blk] * sm) * MLA_SCALE).astype(BF16)

    kvn = _rms(proj(C_CKV, MLA_KV_LORA), kvn_ref[...]).astype(BF16)
    kpe = proj(C_KP, HEAD_PAD) * cm + proj(C_KPR, HEAD_PAD) * sm
    ka = _dot(kvn, wka_ref[...])
    for hd in range(MLA_HEADS):
        blk = slice(hd * HEAD_PAD, (hd + 1) * HEAD_PAD)
        km_ref[:, blk] = (ka[:, blk] + kpe).astype(BF16)
    ones = jnp.ones((SUM_ROWS, hn.shape[0]), F32)
    vt = _dot_nt(wvt_ref[...], kvn)
    pieces = []
    for hd in range(MLA_HEADS):
        pieces += [ones, vt[hd * MLA_V:(hd + 1) * MLA_V]]
    vmt_ref[0] = jnp.concatenate(pieces, axis=0).astype(BF16)

    tqc, tqs = gqc_ref[...] * cg, gqs_ref[...] * sg
    for hd in range(GQA_HEADS):
        x = proj(C_GQ + hd * HEAD_PAD, HEAD_PAD)
        xr = proj(C_GQR + hd * HEAD_PAD, HEAD_PAD)
        n = lax.rsqrt(jnp.sum(x * x, axis=-1, keepdims=True) * (1.0 / GQA_HEAD_DIM) + NORM_EPS)
        qg_ref[:, hd * HEAD_PAD:(hd + 1) * HEAD_PAD] = (n * (x * tqc + xr * tqs) * GQA_SCALE).astype(BF16)
    x = proj(C_GK, 128)
    xr = proj(C_GKR, 128)
    lo = lax.broadcasted_iota(jnp.int32, x.shape, 1) < GQA_HEAD_DIM
    x2 = x * x
    ms0 = jnp.sum(jnp.where(lo, x2, 0.0), axis=-1, keepdims=True) * (1.0 / GQA_HEAD_DIM)
    ms1 = jnp.sum(jnp.where(lo, 0.0, x2), axis=-1, keepdims=True) * (1.0 / GQA_HEAD_DIM)
    n = jnp.where(lo, lax.rsqrt(ms0 + NORM_EPS), lax.rsqrt(ms1 + NORM_EPS))
    kg_ref[...] = (n * (x * (gkc_ref[...] * cg) + xr * (gks_ref[...] * sg))).astype(BF16)
    vg = _dot_nt(wgvt_ref[...], hn)
    vgt_ref[0] = jnp.concatenate([ones, vg[:GQA_HEAD_DIM], ones, vg[GQA_HEAD_DIM:]], axis=0).astype(BF16)

    lq_ref[...] = proj(C_LQ, 128) * GLA_QSCALE
    lk_ref[...] = proj(C_LK, 128)
    lv_ref[...] = proj(C_LV, 256)
    lgate_ref[...] = proj(C_LGATE, 256)
    gg = _dot(proj(C_LG, 128).astype(BF16), wg_ref[...]) + bg_ref[...]
    ls = (jnp.minimum(gg, 0.0) - jnp.log1p(jnp.exp(-jnp.abs(gg)))) * (1.0 / GLA_GATE_NORM)
    gfw_ref[...] = ls[:, :128]
    gbw_ref[...] = ls[:, 128:]


def _mixin_call(h, tile_tab, consts, tabs):
    R = h.shape[0]
    n_tiles = R // TM
    row = lambda w: pl.BlockSpec((TM, w), lambda i, t: (i, 0))
    tabspec = pl.BlockSpec((TM, LANES), lambda i, t: (t[i], 0))
    cs = lambda a: _const_spec(a.shape)
    col3 = lambda r: pl.BlockSpec((1, r, TM), lambda i, t: (i, 0, 0))
    ins = [h, consts["nw"], consts["wbig"], tabs["cm"], tabs["sm"], tabs["cg"], tabs["sg"],
           consts["qn"], consts["kvn"], consts["wqa"], consts["wqb"], consts["wka"], consts["wvt"],
           consts["gqc"], consts["gqs"], consts["gkc"], consts["gks"], consts["wgvt"],
           consts["wg"], consts["bg"]]
    specs = [row(D_MODEL), cs(ins[1]), cs(ins[2]), tabspec, tabspec, tabspec, tabspec] + [cs(a) for a in ins[7:]]
    W6 = MLA_HEADS * HEAD_PAD
    out_shape = (
        jax.ShapeDtypeStruct((R, W6), BF16), jax.ShapeDtypeStruct((R, W6), BF16),
        jax.ShapeDtypeStruct((n_tiles, MLA_HEADS * VROWS_MLA, TM), BF16),
        jax.ShapeDtypeStruct((R, W6), BF16), jax.ShapeDtypeStruct((R, 128), BF16),
        jax.ShapeDtypeStruct((n_tiles, VROWS_GQA, TM), BF16),
        jax.ShapeDtypeStruct((R, 128), F32), jax.ShapeDtypeStruct((R, 128), F32),
        jax.ShapeDtypeStruct((R, 256), F32), jax.ShapeDtypeStruct((R, 128), F32),
        jax.ShapeDtypeStruct((R, 128), F32), jax.ShapeDtypeStruct((R, 256), F32),
    )
    out_specs = (row(W6), row(W6), col3(MLA_HEADS * VROWS_MLA), row(W6), row(128), col3(VROWS_GQA),
                 row(128), row(128), row(256), row(128), row(128), row(256))
    gs = pltpu.PrefetchScalarGridSpec(num_scalar_prefetch=1, grid=(n_tiles,), in_specs=specs, out_specs=out_specs)
    return pl.pallas_call(_mixin_body, grid_spec=gs, out_shape=out_shape,
                          compiler_params=_params(("arbitrary",)), name="mixer_in")(tile_tab, *ins)


def _attn_body(q_ref, k_ref, kmeta_ref, vt_ref, vtmeta_ref, o_ref, *scratch,
               gqa, tq, n_chunks, meta_q, b_axis, p_axis, unroll):
    b = pl.program_id(b_axis)
    p = pl.program_id(p_axis)
    slot = b % (META_BLOCK // N_META)
    rows = lax.broadcasted_iota(jnp.int32, (META_BLOCK, 1), 0)
    meta_valid = (rows // N_META) == slot
    zeros = jnp.zeros((VROWS_MLA, tq), F32)

    def exact_update(carry, s_t, v_t):
        m, acc = carry
        m_new = jnp.maximum(m, jnp.max(s_t, axis=0, keepdims=True))
        p_t = jnp.exp2(s_t - m_new).astype(BF16)
        return m_new, jnp.exp2(m - m_new) * acc + _dot(v_t, p_t)

    def fast_update(carry, s_t, v_t):
        m, acc, excess = carry
        p_t = jnp.exp2(s_t - m).astype(BF16)
        cmax = jnp.max(s_t, axis=0, keepdims=True)
        m_new = jnp.maximum(m, cmax)
        return m_new, jnp.exp2(m - m_new) * (acc + _dot(v_t, p_t)), jnp.maximum(excess, cmax - m)

    def heads(fast):
        update = fast_update if fast else exact_update
        parts = TM // ATTN_TK
        q = [q_ref[:, i * HEAD_PAD:(i + 1) * HEAD_PAD] for i in range(2)]
        ks = [slice(0, HEAD_PAD) if gqa else slice(i * HEAD_PAD, (i + 1) * HEAD_PAD) for i in range(2)]
        if gqa:
            groups = [(2 * p + i) // (GQA_HEADS // GQA_KV_HEADS) for i in range(2)]
            vs = [pl.ds(pl.multiple_of(g * VROWS_MLA, SUM_ROWS), VROWS_MLA) for g in groups]
        else:
            vs = [pl.ds(i * VROWS_MLA, VROWS_MLA) for i in range(2)]

        def scores(i, c):
            return _dot_nt(k_ref[pl.ds(pl.multiple_of(c * ATTN_TK, ATTN_TK), ATTN_TK), ks[i]], q[i])

        def values(i, tile, part):
            return vt_ref[tile, vs[i], part * ATTN_TK:(part + 1) * ATTN_TK]

        def meta_scores(i):
            return jnp.where(meta_valid, _dot_nt(kmeta_ref[:, ks[i]], q[i]), -jnp.inf)

        def init(s0):
            if fast:
                return jnp.max(s0, axis=0, keepdims=True), zeros, jnp.zeros((1, tq), F32)
            return jnp.full((1, tq), -jnp.inf, F32), zeros

        if n_chunks <= ATTN_STATIC_CHUNKS:
            out = []
            for i in range(2):
                s_cur = scores(i, 0)
                carry = init(s_cur)
                for c in range(n_chunks):
                    s_next = scores(i, c + 1) if c + 1 < n_chunks else meta_scores(i)
                    carry = update(carry, s_cur, values(i, c // parts, c % parts))
                    s_cur = s_next
                out.append(update(carry, s_cur, vtmeta_ref[0, vs[i], :]))
            return out

        s_buf = scratch[-1]
        carry = []
        for i in range(2):
            s0 = scores(i, 0)
            s_buf[i, 0] = s0
            carry.append(init(s0))

        def group_step(j, st):
            st = list(st)
            for u in range(unroll):
                c = j * unroll + u
                for i in range(2):
                    s_buf[i, (u + 1) % 2] = scores(i, jnp.minimum(c + 1, n_chunks - 1))
                    st[i] = update(st[i], s_buf[i, u % 2], values(i, j * (unroll // parts) + u // parts, u % parts))
            return tuple(st)

        carry = lax.fori_loop(0, n_chunks // unroll, group_step, tuple(carry))
        return [update(carry[i], meta_scores(i), vtmeta_ref[0, vs[i], :]) for i in range(2)]

    if n_chunks <= ATTN_STATIC_CHUNKS:
        fast = heads(True)
        worst = jnp.max(jnp.maximum(fast[0][2], fast[1][2]))
        accs = lax.cond(worst > ATTN_MAX_EXCESS,
                        lambda: tuple(c[1] for c in heads(False)),
                        lambda: tuple(c[1] for c in fast))
    else:
        accs = [c[1] for c in heads(False)]
    halves = [acc[SUM_ROWS:] / acc[0:1] for acc in accs]
    o = jnp.concatenate(halves, axis=0).T.astype(BF16)
    if meta_q:
        stage = scratch[0]

        @pl.when(slot == 0)
        def _():
            o_ref[...] = jnp.zeros(o_ref.shape, o_ref.dtype)

        stage[...] = o
        off = pl.multiple_of(slot * N_META, N_META)
        o_ref[pl.ds(off, N_META), :] = stage[pl.ds(off, N_META), :]
    else:
        o_ref[...] = o


def _attn_call(q, k, vt, g, *, gqa, meta_q, tq):
    B, S = g["B"], g["S"]
    n_chunks = S // ATTN_TK
    n_tiles = S // TM
    kw = HEAD_PAD if gqa else 2 * HEAD_PAD
    vrows = VROWS_GQA if gqa else 2 * VROWS_MLA
    main_blk = g["main_off"] // S
    meta_blk = g["meta_off"] // META_BLOCK
    per_blk = META_BLOCK // N_META
    pk = (lambda p: 0) if gqa else (lambda p: p)
    if meta_q:
        grid = (3, B)
        ax = lambda f: (lambda p, b: f(b, p, 0))
        tq = META_BLOCK
        q_map = lambda b, p, t: (meta_blk + b // per_blk, p)
        o_map = lambda b, p, t: (b // per_blk, p)
        out_rows = pl.cdiv(B * N_META, META_BLOCK) * META_BLOCK
        sem = ("arbitrary", "arbitrary")
        scratch = [pltpu.VMEM((META_BLOCK, HEAD_PAD), BF16)]
        b_axis, p_axis = 1, 0
    else:
        nq = S // tq
        grid = (B, 3, nq)
        ax = lambda f: f
        q_map = lambda b, p, t: ((g["main_off"] + b * S) // tq + t, p)
        o_map = lambda b, p, t: (b * nq + t, p)
        out_rows = B * S
        sem = ("arbitrary", "arbitrary", "arbitrary")
        scratch = []
        b_axis, p_axis = 0, 1
    in_specs = [
        pl.BlockSpec((tq, 2 * HEAD_PAD), ax(q_map)),
        pl.BlockSpec((S, kw), ax(lambda b, p, t: (main_blk + b, pk(p)))),
        pl.BlockSpec((META_BLOCK, kw), ax(lambda b, p, t: (meta_blk + b // per_blk, pk(p)))),
        pl.BlockSpec((n_tiles, vrows, TM), ax(lambda b, p, t: (main_blk + b, pk(p), 0))),
        pl.BlockSpec((1, vrows, META_BLOCK),
                     ax(lambda b, p, t: (g["meta_off"] // TM + (b * N_META) // TM, pk(p),
                                         ((b * N_META) % TM) // META_BLOCK))),
    ]
    unroll = None
    if n_chunks > ATTN_STATIC_CHUNKS:
        step = 2 * TM // ATTN_TK
        unroll = max(u for u in range(step, ATTN_UNROLL + 1, step) if n_chunks % u == 0)
        scratch = scratch + [pltpu.VMEM((2, 2, ATTN_TK, tq), F32)]
    body = functools.partial(_attn_body, gqa=gqa, tq=tq, n_chunks=n_chunks, meta_q=meta_q,
                             b_axis=b_axis, p_axis=p_axis, unroll=unroll)
    return pl.pallas_call(
        body, grid=grid, in_specs=in_specs,
        out_specs=pl.BlockSpec((tq, HEAD_PAD), ax(o_map)),
        out_shape=jax.ShapeDtypeStruct((out_rows, 3 * HEAD_PAD), BF16),
        scratch_shapes=scratch, compiler_params=_params(sem),
        name=("gqa" if gqa else "mla") + ("_metaq_" if meta_q else "_") + g["name"],
    )(q, k, k, vt, vt)


def _gla_consts():
    c = GLA_CHUNK
    r = lax.broadcasted_iota(jnp.int32, (c, c), 0)
    s = lax.broadcasted_iota(jnp.int32, (c, c), 1)
    r4 = lax.broadcasted_iota(jnp.int32, (GLA_HEADS * c, c), 0) % c
    s4 = lax.broadcasted_iota(jnp.int32, (GLA_HEADS * c, c), 1)
    kl = lax.broadcasted_iota(jnp.int32, (1, GLA_HEADS * GLA_DK), 1) // GLA_DK
    vl = lax.broadcasted_iota(jnp.int32, (1, GLA_HEADS * GLA_DV), 1) // GLA_DV
    kr = lax.broadcasted_iota(jnp.int32, (GLA_HEADS * GLA_DK, GLA_HEADS * GLA_DV), 0) // GLA_DK
    vc = lax.broadcasted_iota(jnp.int32, (GLA_HEADS * GLA_DK, GLA_HEADS * GLA_DV), 1) // GLA_DV
    return dict(
        tril=(s <= r).astype(BF16), triu=(s >= r).astype(BF16),
        mask4_fw=s4 <= r4, mask4_bw=s4 >= r4,
        kmask=[(kl == h).astype(F32) for h in range(GLA_HEADS)],
        vmask=[(vl == h).astype(F32) for h in range(GLA_HEADS)],
        blockdiag=(kr == vc).astype(F32),
        ones=jnp.ones((c, GLA_HEADS * GLA_DV), BF16),
    )


def _gla_chunks(qs, ks, vs, gs, state, cst, fwd):
    c, n = GLA_CHUNK, len(qs)
    nk = GLA_HEADS * GLA_DK
    tri = cst["tril"] if fwd else cst["triu"]
    hilo = []
    for g in gs:
        hi = g.astype(BF16)
        hilo.append(jnp.concatenate([hi, (g - hi.astype(F32)).astype(BF16)], axis=1))
    bcum = [_dot(tri, x) for x in hilo]
    bcum = [x[:, :nk] + x[:, nk:] for x in bcum]
    ltot = [_dot_tn(x, cst["ones"]) for x in hilo]
    decay = [jnp.exp(x[:nk] + x[nk:]) for x in ltot]
    qe = [q * jnp.exp(b) for q, b in zip(qs, bcum)]
    ke = [(k * jnp.exp(-b)).astype(BF16) for k, b in zip(ks, bcum)]
    kd = [(k * jnp.exp((b[c - 1:c] if fwd else b[0:1]) - b)).astype(BF16) for k, b in zip(ks, bcum)]
    vb = [v.astype(BF16) for v in vs]
    qstack = [jnp.concatenate([x * cst["kmask"][h] for h in range(GLA_HEADS)], axis=0).astype(BF16) for x in qe]
    mask = cst["mask4_fw"] if fwd else cst["mask4_bw"]
    att = [jnp.where(mask, _dot_nt(x, y), 0.0).astype(BF16) for x, y in zip(qstack, ke)]
    full = [_dot(x, y) for x, y in zip(att, vb)]
    dstate = [_dot_tn(x, y) * cst["blockdiag"] for x, y in zip(kd, vb)]
    states = []
    for i in range(n):
        states.append(state.astype(BF16))
        state = decay[i] * state + dstate[i]
    outs = []
    for i in range(n):
        o = _dot(qe[i].astype(BF16), states[i])
        for h in range(GLA_HEADS):
            o = o + full[i][h * c:(h + 1) * c] * cst["vmask"][h]
        outs.append(o)
    return outs, state


def _meta_chunk(ref):
    x = ref[...]
    return jnp.concatenate([jnp.zeros((GLA_CHUNK - N_META, x.shape[1]), x.dtype), x], axis=0)


def _gla_fw_body(q_ref, k_ref, v_ref, g_ref, qm_ref, km_ref, vm_ref, gm_ref, o_ref, om_ref, state):
    cst = _gla_consts()

    @pl.when(pl.program_id(1) == 0)
    def _():
        (o,), s = _gla_chunks([_meta_chunk(qm_ref)], [_meta_chunk(km_ref)], [_meta_chunk(vm_ref)],
                              [_meta_chunk(gm_ref)], jnp.zeros(state.shape, F32), cst, True)
        om_ref[...] = o[GLA_CHUNK - N_META:]
        state[...] = s

    sls = [slice(c * GLA_CHUNK, (c + 1) * GLA_CHUNK) for c in range(TM // GLA_CHUNK)]
    outs, s = _gla_chunks([q_ref[sl, :] for sl in sls], [k_ref[sl, :] for sl in sls], [v_ref[sl, :] for sl in sls],
                          [g_ref[sl, :] for sl in sls], state[...], cst, True)
    for sl, o in zip(sls, outs):
        o_ref[sl, :] = o
    state[...] = s


def _gla_finish(o, gate, w, bd):
    x2 = o * o
    hi = x2.astype(BF16)
    lo = (x2 - hi.astype(F32)).astype(BF16)
    ms = _dot(hi, bd) + _dot(lo, bd)
    return (o * lax.rsqrt(ms + NORM_EPS) * w * (gate * jax.nn.sigmoid(gate))).astype(BF16)


def _gla_bw_body(q_ref, k_ref, v_ref, g_ref, ofw_ref, gate_ref, qm_ref, km_ref, vm_ref, gm_ref, ofwm_ref, gatem_ref,
                 w_ref, o_ref, om_ref, state):
    cst = _gla_consts()
    t = pl.program_id(1)
    r = lax.broadcasted_iota(jnp.int32, (GLA_HEADS * GLA_DV,) * 2, 0) // GLA_DV
    c_ = lax.broadcasted_iota(jnp.int32, (GLA_HEADS * GLA_DV,) * 2, 1) // GLA_DV
    bd = jnp.where(r == c_, 1.0 / GLA_DV, 0.0).astype(BF16)

    @pl.when(t == 0)
    def _():
        state[...] = jnp.zeros(state.shape, F32)

    sls = [slice(c * GLA_CHUNK, (c + 1) * GLA_CHUNK) for c in reversed(range(TM // GLA_CHUNK))]
    outs, s = _gla_chunks([q_ref[sl, :] for sl in sls], [k_ref[sl, :] for sl in sls], [v_ref[sl, :] for sl in sls],
                          [g_ref[sl, :] for sl in sls], state[...], cst, False)
    for sl, o in zip(sls, outs):
        o_ref[sl, :] = _gla_finish(o + ofw_ref[sl, :], gate_ref[sl, :], w_ref[...], bd)
    state[...] = s

    @pl.when(t == pl.num_programs(1) - 1)
    def _():
        (o,), _ = _gla_chunks([_meta_chunk(qm_ref)], [_meta_chunk(km_ref)], [_meta_chunk(vm_ref)],
                              [_meta_chunk(gm_ref)], s, cst, False)
        om_ref[...] = _gla_finish(o[GLA_CHUNK - N_META:] + ofwm_ref[...], gatem_ref[...], w_ref[...], bd)


def _gla_call(lq, lk, lv, gfw, gbw, lgate, wnorm, g):
    B, S = g["B"], g["S"]
    nt = S // TM
    main0 = g["main_off"] // TM
    meta0 = g["meta_off"] // N_META
    mrow = lambda w, rev: pl.BlockSpec(
        (TM, w), (lambda b, t: (main0 + b * nt + (nt - 1 - t), 0)) if rev else (lambda b, t: (main0 + b * nt + t, 0)))
    meta = lambda w: pl.BlockSpec((N_META, w), lambda b, t: (meta0 + b, 0))
    orow = lambda w, rev: pl.BlockSpec(
        (TM, w), (lambda b, t: (b * nt + (nt - 1 - t), 0)) if rev else (lambda b, t: (b * nt + t, 0)))
    ometa = lambda w: pl.BlockSpec((N_META, w), lambda b, t: (b, 0))
    sem = _params(("arbitrary", "arbitrary"))
    state = [pltpu.VMEM((GLA_HEADS * GLA_DK, GLA_HEADS * GLA_DV), F32)]
    ofw, ofw_meta = pl.pallas_call(
        _gla_fw_body, grid=(B, nt),
        in_specs=[mrow(128, False), mrow(128, False), mrow(256, False), mrow(128, False),
                  meta(128), meta(128), meta(256), meta(128)],
        out_specs=(orow(256, False), ometa(256)),
        out_shape=(jax.ShapeDtypeStruct((B * S, 256), F32), jax.ShapeDtypeStruct((B * N_META, 256), F32)),
        scratch_shapes=state, compiler_params=sem, name="gla_fw_" + g["name"],
    )(lq, lk, lv, gfw, lq, lk, lv, gfw)
    o, o_meta = pl.pallas_call(
        _gla_bw_body, grid=(B, nt),
        in_specs=[mrow(128, True), mrow(128, True), mrow(256, True), mrow(128, True), orow(256, True), mrow(256, True),
                  meta(128), meta(128), meta(256), meta(128), ometa(256), meta(256), _const_spec(wnorm.shape)],
        out_specs=(orow(256, True), ometa(256)),
        out_shape=(jax.ShapeDtypeStruct((B * S, 256), BF16), jax.ShapeDtypeStruct((B * N_META, 256), BF16)),
        scratch_shapes=state, compiler_params=sem, name="gla_bw_" + g["name"],
    )(lq, lk, lv, gbw, ofw, lgate, lq, lk, lv, gbw, ofw_meta, lgate, wnorm)
    return o, o_meta


def _rot_cols(w, n):
    s = w.shape
    w4 = w.reshape(s[:-1] + (s[-1] // (2 * n), 2, n))
    return jnp.stack([-w4[..., 1, :], w4[..., 0, :]], axis=-2).reshape(s)


def _swap_halves(w, n):
    w3 = w.reshape(-1, 2, n)
    return w3[:, ::-1, :].reshape(w.shape)


def _layout(bp, sp, bs, ss):
    up = lambda x: -(-x // TM) * TM
    lay = {"p": dict(name="p", B=bp, S=sp, main_off=0, main_rows=bp * sp),
           "s": dict(name="s", B=bs, S=ss, main_off=bp * sp, main_rows=bs * ss)}
    lay["p"]["meta_off"] = bp * sp + bs * ss
    lay["s"]["meta_off"] = lay["p"]["meta_off"] + up(bp * N_META)
    lay["R"] = lay["s"]["meta_off"] + up(bs * N_META)
    for g in (lay["p"], lay["s"]):
        assert g["S"] % TM == 0 and g["main_off"] % g["S"] == 0 and g["S"] % GRID_W == 0
    return lay


def _tile_table(lay):
    smax = max(lay["p"]["S"], lay["s"]["S"])
    tab = []
    for g in (lay["p"], lay["s"]):
        tab += [t % (g["S"] // TM) for t in range(g["main_rows"] // TM)]
    tab += [smax // TM] * ((lay["R"] - lay["p"]["meta_off"]) // TM)
    return jnp.asarray(np.asarray(tab, np.int32)), smax


def _rope_tables(smax):
    r = jnp.arange(smax, dtype=jnp.int32)
    meta = jnp.arange(TM, dtype=jnp.int32) % N_META
    zero = jnp.zeros((TM,), jnp.int32)
    pos = jnp.concatenate([r + N_META, meta]).astype(F32)
    row = jnp.concatenate([r // GRID_W, zero]).astype(F32)
    col = jnp.concatenate([r % GRID_W, zero]).astype(F32)
    inv_m = ROPE_THETA ** (-jnp.arange(0, MLA_ROPE, 2, dtype=F32) / MLA_ROPE)
    half = GQA_HEAD_DIM // 2
    inv_g = ROPE_THETA ** (-jnp.arange(0, half, 2, dtype=F32) / half)
    a1, ar, ac = pos[:, None] * inv_m, row[:, None] * inv_g, col[:, None] * inv_g
    n = pos.shape[0]
    one, zer = jnp.ones((n, MLA_NOPE), F32), jnp.zeros((n, MLA_NOPE), F32)
    pad = jnp.zeros((n, HEAD_PAD - MLA_NOPE - MLA_ROPE), F32)
    cm = jnp.concatenate([one, jnp.cos(a1), jnp.cos(a1), pad], axis=1)
    sm = jnp.concatenate([zer, jnp.sin(a1), jnp.sin(a1), pad], axis=1)
    cg = jnp.concatenate([jnp.cos(ar), jnp.cos(ar), jnp.cos(ac), jnp.cos(ac)] * 2, axis=1)
    sg = jnp.concatenate([jnp.sin(ar), jnp.sin(ar), jnp.sin(ac), jnp.sin(ac)] * 2, axis=1)
    return dict(cm=cm, sm=sm, cg=cg, sg=sg)


def _mixer_consts(i, mix_norm, w_in, mla_q_norm, mla_w_uq, mla_kv_norm, mla_w_ukv, gqa_q_norm, gqa_k_norm,
                  gate_fw_w, gate_fw_b, gate_bw_w, gate_bw_b):
    w = w_in[i]
    z = lambda n: jnp.zeros((D_MODEL, n), F32)
    o = np.cumsum([0, MLA_Q_LORA, MLA_KV_LORA, MLA_ROPE, 384, 128, 128, 128, 128, 256, 16, 16, 256])
    seg = lambda j: w[:, o[j]:o[j + 1]]
    w_kr, w_gq, w_gk = seg(2), seg(3), seg(4)
    w_gqr = _rot_cols(w_gq, 16)
    tail = HEAD_PAD - MLA_NOPE - MLA_ROPE

    def gq_blocks(m):
        out = []
        for h in range(GQA_HEADS):
            blk = m[:, h * GQA_HEAD_DIM:(h + 1) * GQA_HEAD_DIM]
            out += [blk, z(GQA_HEAD_DIM)] if h < GQA_HEADS // GQA_KV_HEADS else [z(GQA_HEAD_DIM), blk]
        return out

    wbig = jnp.concatenate(
        [seg(0), seg(1), z(MLA_NOPE), w_kr, z(tail), z(MLA_NOPE), _rot_cols(w_kr, 16), z(tail)]
        + gq_blocks(w_gq) + gq_blocks(w_gqr)
        + [w_gk, _rot_cols(w_gk, 16), seg(6), seg(7), seg(8), seg(11), seg(9), seg(10), z(128 - 2 * GLA_GATE_RANK)],
        axis=1).astype(BF16)
    assert wbig.shape[1] == C_END

    uq = mla_w_uq[i].reshape(MLA_Q_LORA, MLA_HEADS, MLA_NOPE + MLA_ROPE)
    zq = jnp.zeros((MLA_Q_LORA, MLA_HEADS, tail), F32)
    wqa = jnp.concatenate([uq, zq], axis=2).reshape(MLA_Q_LORA, -1).astype(BF16)
    wqb = jnp.concatenate([jnp.zeros((MLA_Q_LORA, MLA_HEADS, MLA_NOPE), F32),
                           _rot_cols(uq[:, :, MLA_NOPE:], 16), zq], axis=2).reshape(MLA_Q_LORA, -1).astype(BF16)
    ukv = mla_w_ukv[i].reshape(MLA_KV_LORA, MLA_HEADS, MLA_NOPE + MLA_V)
    wka = jnp.concatenate([ukv[:, :, :MLA_NOPE], jnp.zeros((MLA_KV_LORA, MLA_HEADS, HEAD_PAD - MLA_NOPE), F32)],
                          axis=2).reshape(MLA_KV_LORA, -1).astype(BF16)
    wvt = ukv[:, :, MLA_NOPE:].reshape(MLA_KV_LORA, -1).T.astype(BF16)
    wgvt = seg(5).T.astype(BF16)

    gq, gk = gqa_q_norm[i], gqa_k_norm[i]
    gqs, gks = _swap_halves(gq, 16), _swap_halves(gk, 16)
    two = lambda v: jnp.concatenate([v, v])[None, :]
    wg = jnp.zeros((128, 256), F32)
    wg = wg.at[:GLA_GATE_RANK, :128].set(gate_fw_w[i]).at[GLA_GATE_RANK:2 * GLA_GATE_RANK, 128:].set(gate_bw_w[i])
    return dict(
        nw=mix_norm[i][None, :], wbig=wbig, qn=mla_q_norm[i][None, :], kvn=mla_kv_norm[i][None, :],
        wqa=wqa, wqb=wqb, wka=wka, wvt=wvt, gqc=two(gq), gqs=two(gqs), gkc=two(gk), gks=two(gks),
        wgvt=wgvt, wg=wg.astype(BF16), bg=jnp.concatenate([gate_fw_b[i], gate_bw_b[i]])[None, :])


def _row_pieces(lay, parts):
    pieces = [parts["p"][0], parts["s"][0]]
    for name, nxt in (("p", lay["s"]["meta_off"]), ("s", lay["R"])):
        m = parts[name][1][:lay[name]["B"] * N_META]
        pad = nxt - lay[name]["meta_off"] - m.shape[0]
        pieces.append(jnp.concatenate([m, jnp.zeros((pad, m.shape[1]), m.dtype)], axis=0))
    return pieces


def kernel(x_prompt, x_sample, meta_tokens, final_norm, ffn1_norm, ffn1_w_in, ffn1_w_out, mix_norm, w_in, w_out,
           mla_q_norm, mla_w_uq, mla_kv_norm, mla_w_ukv, gqa_q_norm, gqa_k_norm, gla_gate_fw_w, gla_gate_fw_b,
           gla_gate_bw_w, gla_gate_bw_b, gla_out_norm, ffn2_norm, ffn2_w_in, ffn2_w_out):
    bp, sp, _ = x_prompt.shape
    bs, ss, _ = x_sample.shape
    depth = w_in.shape[0]
    lay = _layout(bp, sp, bs, ss)
    tile_tab, smax = _tile_table(lay)
    tabs = _rope_tables(smax)
    h = _row_pieces(lay, {"p": (x_prompt.reshape(-1, D_MODEL), jnp.tile(meta_tokens, (bp, 1))),
                          "s": (x_sample.reshape(-1, D_MODEL), jnp.tile(meta_tokens, (bs, 1)))})
    wnorm = jnp.tile(gla_out_norm, (1, GLA_HEADS))
    n_mla = MLA_HEADS * MLA_V
    tq = {"p": min(2 * TM, sp), "s": min(TM, ss)}

    def ffn_w(w_i, w_o):
        return w_i[:, :D_FF].astype(BF16), w_i[:, D_FF:].astype(BF16), w_o.astype(BF16)

    y = None
    for i in range(depth):
        h = _ffn_call(h, ffn1_norm[i][None, :], *ffn_w(ffn1_w_in[i], ffn1_w_out[i]), lay)
        consts = _mixer_consts(i, mix_norm, w_in, mla_q_norm, mla_w_uq, mla_kv_norm, mla_w_ukv, gqa_q_norm,
                               gqa_k_norm, gla_gate_fw_w, gla_gate_fw_b, gla_gate_bw_w, gla_gate_bw_b)
        qm, km, vmt, qg, kg, vgt, lq, lk, lv, gfw, gbw, lgate = _mixin_call(h, tile_tab, consts, tabs)
        o_mla, o_gqa, o_gla = {}, {}, {}
        for name in ("p", "s"):
            g = lay[name]
            o_mla[name] = (_attn_call(qm, km, vmt, g, gqa=False, meta_q=False, tq=tq[name]),
                           _attn_call(qm, km, vmt, g, gqa=False, meta_q=True, tq=None))
            o_gqa[name] = (_attn_call(qg, kg, vgt, g, gqa=True, meta_q=False, tq=tq[name]),
                           _attn_call(qg, kg, vgt, g, gqa=True, meta_q=True, tq=None))
            o_gla[name] = _gla_call(lq, lk, lv, gfw, gbw, lgate, wnorm[i][None, :], g)
        wo = w_out[i].astype(BF16)
        proj = (_row_pieces(lay, o_mla), _row_pieces(lay, o_gqa), _row_pieces(lay, o_gla),
                wo[:n_mla], wo[n_mla:2 * n_mla], wo[2 * n_mla:])
        last = i == depth - 1
        out = _ffn_call(h, ffn2_norm[i][None, :], *ffn_w(ffn2_w_in[i], ffn2_w_out[i]), lay, proj=proj,
                        final=final_norm[None, :] if last else None)
        if last:
            y = out
        else:
            h = out
    return y[0].reshape(bp, sp, D_MODEL), y[1].reshape(bs, ss, D_MODEL)
```
